```python
import jax, jax.numpy as jnp
from jax import lax
import numpy as np

D_MODEL = 2048
BATCH = 2
SEQ = 4096
DEPTH = 4
DEC_BATCH = 8
DEC_SEQ = 1
PAST_LEN = 16384
PAGE_SIZE = 128

HEAD_DIM = 128
N_HEADS = D_MODEL // HEAD_DIM
DIL_GROUPS = ((128, 1), (512, 4), (2048, 16))
N_DIL = len(DIL_GROUPS)
DIL_COLS = 3 * N_DIL + 1
MOBA_COLS = 4
MOBA_BLOCK = 256
MOBA_TOPK = 3
MOBA_QCHUNK = 32
ROPE_THETA = 10000.0
RMS_EPS = 1e-6
N_MIXERS = 2
N_DIL_LAYERS = (DEPTH + 1) // 2
N_MOBA_LAYERS = DEPTH // 2
SCALE = HEAD_DIM ** -0.5
NEG = -1e30

kernel_name = 'dilated_moba_hybrid_decoder_step'


def rmsnorm(x, g):
    xf = x.astype(jnp.float32)
    y = xf * lax.rsqrt(jnp.mean(xf * xf, axis=-1, keepdims=True) + RMS_EPS) * g.astype(jnp.float32)
    return y.astype(x.dtype)


def rope(x, pos):
    half = HEAD_DIM // 2
    inv = ROPE_THETA ** (-jnp.arange(half, dtype=jnp.float32) / half)
    ang = pos.astype(jnp.float32)[:, None] * inv[None, :]
    cos = jnp.cos(ang)[:, None, :]
    sin = jnp.sin(ang)[:, None, :]
    xf = x.astype(jnp.float32)
    x1, x2 = xf[..., :half], xf[..., half:]
    return jnp.concatenate([x1 * cos - x2 * sin, x2 * cos + x1 * sin], axis=-1).astype(x.dtype)


def attend(s, v, spec):
    m = jnp.max(s, axis=-1, keepdims=True)
    p = jnp.exp(s - m)
    l = jnp.sum(p, axis=-1, keepdims=True)
    o = jnp.einsum(spec, (p / l).astype(v.dtype), v, preferred_element_type=jnp.float32)
    return o, (m + jnp.log(l))[..., 0]


def merge_lse(outs, lses):
    w = jax.nn.softmax(jnp.stack(lses, axis=0), axis=0)
    o = w[0][..., None] * outs[0]
    for g in range(1, len(outs)):
        o = o + w[g][..., None] * outs[g]
    return o


def project(h, w_in, n_cols):
    z = jnp.einsum('bsd,de->bse', h, w_in)
    return z.reshape(h.shape[:-1] + (n_cols, N_HEADS, HEAD_DIM))


def gate_out(o, z_gate, w_out):
    u = o.astype(z_gate.dtype) * jax.nn.silu(z_gate)
    u = u.reshape(u.shape[:-2] + (N_HEADS * HEAD_DIM,))
    return jnp.einsum('bse,ed->bsd', u, w_out)


def dilated_prompt(q, k, v, window, dil):
    B, S, H, Dh = q.shape
    span = window // dil
    L = S // dil
    nb = -(-L // span)
    lp = nb * span

    def split(t):
        t = jnp.moveaxis(t.reshape((B, L, dil) + t.shape[2:]), 2, 1)
        t = jnp.pad(t, ((0, 0), (0, 0), (0, lp - L)) + ((0, 0),) * (t.ndim - 3))
        return t.reshape((B, dil, nb, span) + t.shape[3:])

    def unsplit(t):
        t = t.reshape((B, dil, lp) + t.shape[4:])[:, :, :L]
        return jnp.moveaxis(t, 1, 2).reshape((B, S) + t.shape[3:])

    qb, kb, vb = split(q), split(k), split(v)
    shift = lambda t: jnp.pad(t, ((0, 0), (0, 0), (1, 0), (0, 0), (0, 0), (0, 0)))[:, :, :-1]
    kc = jnp.concatenate([shift(kb), kb], axis=3)
    vc = jnp.concatenate([shift(vb), vb], axis=3)
    s = jnp.einsum('brnqhd,brnkhd->brnhqk', qb, kc, preferred_element_type=jnp.float32) * SCALE
    qi = jnp.arange(span)[:, None] + span
    ki = jnp.arange(2 * span)[None, :]
    band = ((qi - ki) >= 0) & ((qi - ki) <= span)
    has_prev = (jnp.arange(nb) > 0)[:, None, None] | (ki >= span)[None]
    mask = band[None] & has_prev
    o, lse = attend(jnp.where(mask[None, None, :, None], s, NEG), vc, 'brnhqk,brnkhd->brnqhd')
    lse = jnp.swapaxes(lse, 3, 4)
    return unsplit(o), unsplit(lse)


def dilated_sample(q, k_new, v_new, buf, window, dil):
    T = q.shape[1]
    wc = buf.shape[1]
    span = window // dil
    k_all = jnp.concatenate([buf[:, :, 0], k_new], axis=1)
    v_all = jnp.concatenate([buf[:, :, 1], v_new], axis=1)
    idx = wc + jnp.arange(T)[:, None] - dil * jnp.arange(span + 1)[None, :]
    valid = idx >= 0
    idx = jnp.maximum(idx, 0)
    kg = k_all[:, idx]
    vg = v_all[:, idx]
    s = jnp.einsum('bthd,btkhd->bthk', q, kg, preferred_element_type=jnp.float32) * SCALE
    return attend(jnp.where(valid[None, :, None, :], s, NEG), vg, 'bthk,btkhd->bthd')


def moba_prompt(q, k, v):
    B, S, H, Dh = q.shape
    nb = -(-S // MOBA_BLOCK)
    sp = nb * MOBA_BLOCK

    def blocks(t):
        t = jnp.pad(t, ((0, 0), (0, sp - S), (0, 0), (0, 0)))
        return t.reshape(B, nb, MOBA_BLOCK, H, Dh).transpose(0, 3, 1, 2, 4)

    qb, kb, vb = blocks(q), blocks(k), blocks(v)
    s = jnp.einsum('bhnqd,bhnkd->bhnqk', qb, kb, preferred_element_type=jnp.float32) * SCALE
    causal = jnp.tril(jnp.ones((MOBA_BLOCK, MOBA_BLOCK), dtype=bool))
    o_own, l_own = attend(jnp.where(causal, s, NEG), vb, 'bhnqk,bhnkd->bhnqd')
    o_own = o_own.reshape(B, H, sp, Dh)
    l_own = l_own.reshape(B, H, sp)
    n_sel = min(MOBA_TOPK, nb - 1)
    if n_sel > 0:
        qh = qb.reshape(B, H, sp, Dh)
        kmean = jnp.mean(kb, axis=3, dtype=jnp.float32)
        gate = jnp.einsum('bhsd,bhnd->bhsn', qh.astype(jnp.float32), kmean)
        cur = jnp.arange(sp) // MOBA_BLOCK
        past = jnp.arange(nb)[None, :] < cur[:, None]
        top = lax.top_k(jnp.where(past, gate, NEG), n_sel)[1]
        ok = top < cur[:, None]
        nc = sp // MOBA_QCHUNK

        def chunked(t):
            return jnp.moveaxis(t.reshape(B, H, nc, MOBA_QCHUNK, *t.shape[3:]), 2, 0)

        take = jax.vmap(jax.vmap(lambda blk, ids: blk[ids]))

        def select(args):
            qc, ic, okc = args
            kg = take(kb, ic).reshape(B, H, MOBA_QCHUNK, n_sel * MOBA_BLOCK, Dh)
            vg = take(vb, ic).reshape(B, H, MOBA_QCHUNK, n_sel * MOBA_BLOCK, Dh)
            sc = jnp.einsum('bhqd,bhqkd->bhqk', qc, kg, preferred_element_type=jnp.float32) * SCALE
            keep = jnp.repeat(okc, MOBA_BLOCK, axis=-1)
            return attend(jnp.where(keep, sc, NEG), vg, 'bhqk,bhqkd->bhqd')

        o_sel, l_sel = lax.map(select, (chunked(qh), chunked(top), chunked(ok)))
        o_sel = jnp.moveaxis(o_sel, 0, 2).reshape(B, H, sp, Dh)
        l_sel = jnp.moveaxis(l_sel, 0, 2).reshape(B, H, sp)
        o = merge_lse([o_own, o_sel], [l_own, l_sel])
    else:
        o = o_own
    return o.transpose(0, 2, 1, 3)[:, :S]


def moba_sample(q, k_new, v_new, pool, page_table):
    DB, T, H, Dh = q.shape
    n_pages = page_table.shape[1]
    ppb = MOBA_BLOCK // PAGE_SIZE
    n_full = n_pages // ppb
    r_own = (n_pages - n_full * ppb) * PAGE_SIZE
    if r_own > 0:
        own = pool[page_table[:, n_full * ppb:]].reshape(DB, r_own, 2, H, Dh)
        k_own = jnp.concatenate([own[:, :, 0], k_new], axis=1)
        v_own = jnp.concatenate([own[:, :, 1], v_new], axis=1)
    else:
        k_own, v_own = k_new, v_new
    j = jnp.arange(r_own + T)[None, :]
    t = jnp.arange(T)[:, None]
    own_mask = (j < r_own) | ((j - r_own) <= t)
    s = jnp.einsum('bthd,bkhd->bthk', q, k_own, preferred_element_type=jnp.float32) * SCALE
    o_own, l_own = attend(jnp.where(own_mask[None, :, None, :], s, NEG), v_own, 'bthk,bkhd->bthd')
    n_sel = min(MOBA_TOPK, n_full)
    if n_sel == 0:
        return o_own
    k_past = pool[page_table[:, :n_full * ppb], :, 0]
    kmean = jnp.mean(k_past.reshape(DB, n_full, MOBA_BLOCK, H, Dh), axis=2, dtype=jnp.float32)
    gate = jnp.einsum('bthd,bnhd->bthn', q.astype(jnp.float32), kmean)
    top = lax.top_k(gate, n_sel)[1]
    logical = top[..., None] * ppb + jnp.arange(ppb)
    phys = jax.vmap(lambda pt, lg: pt[lg])(page_table, logical)
    ph = phys[..., None]
    rows = jnp.arange(PAGE_SIZE)
    hh = jnp.arange(H)[None, None, :, None, None, None]
    kg = pool[ph, rows, 0, hh].reshape(DB, T, H, n_sel * MOBA_BLOCK, Dh)
    vg = pool[ph, rows, 1, hh].reshape(DB, T, H, n_sel * MOBA_BLOCK, Dh)
    sc = jnp.einsum('bthd,bthkd->bthk', q, kg, preferred_element_type=jnp.float32) * SCALE
    o_sel, l_sel = attend(sc, vg, 'bthk,bthkd->bthd')
    return merge_lse([o_own, o_sel], [l_own, l_sel])


def setup_inputs(seed: int = 0) -> dict:
    key = jax.random.key(seed)
    ks = jax.random.split(key, 16)
    width = N_HEADS * HEAD_DIM
    n_pages = PAST_LEN // PAGE_SIZE
    n_used = DEC_BATCH * n_pages
    n_pool = n_used + max(1, n_used // 4)
    page_table = jax.random.permutation(ks[0], n_pool)[:n_used].reshape(DEC_BATCH, n_pages).astype(jnp.int32)
    x_prompt = jax.random.normal(ks[1], (BATCH, SEQ, D_MODEL), jnp.float32)
    x_sample = jax.random.normal(ks[2], (DEC_BATCH, DEC_SEQ, D_MODEL), jnp.float32)
    cache_dil0 = jax.random.normal(ks[3], (N_DIL_LAYERS, DEC_BATCH, min(DIL_GROUPS[0][0], PAST_LEN), 2, N_HEADS, HEAD_DIM), jnp.float32)
    cache_dil1 = jax.random.normal(ks[4], (N_DIL_LAYERS, DEC_BATCH, min(DIL_GROUPS[1][0], PAST_LEN), 2, N_HEADS, HEAD_DIM), jnp.float32)
    cache_dil2 = jax.random.normal(ks[5], (N_DIL_LAYERS, DEC_BATCH, min(DIL_GROUPS[2][0], PAST_LEN), 2, N_HEADS, HEAD_DIM), jnp.float32)
    cache_moba = jax.random.normal(ks[6], (N_MOBA_LAYERS, n_pool, PAGE_SIZE, 2, N_HEADS, HEAD_DIM), jnp.float32)
    norm_dil = 1.0 + 0.01 * jax.random.normal(ks[7], (N_DIL_LAYERS, D_MODEL), jnp.float32)
    w_in_dil = jax.random.normal(ks[8], (N_DIL_LAYERS, D_MODEL, DIL_COLS * width), jnp.float32) * D_MODEL ** -0.5
    w_out_dil = jax.random.normal(ks[9], (N_DIL_LAYERS, width, D_MODEL), jnp.float32) * width ** -0.5
    norm_moba = 1.0 + 0.01 * jax.random.normal(ks[10], (N_MOBA_LAYERS, D_MODEL), jnp.float32)
    w_in_moba = jax.random.normal(ks[11], (N_MOBA_LAYERS, D_MODEL, MOBA_COLS * width), jnp.float32) * D_MODEL ** -0.5
    w_out_moba = jax.random.normal(ks[12], (N_MOBA_LAYERS, width, D_MODEL), jnp.float32) * width ** -0.5
    final_norm = 1.0 + 0.01 * jax.random.normal(ks[13], (D_MODEL,), jnp.float32)
    return {'x_prompt': x_prompt, 'x_sample': x_sample,
            'cache_dil0': cache_dil0, 'cache_dil1': cache_dil1, 'cache_dil2': cache_dil2,
            'cache_moba': cache_moba, 'page_table': page_table,
            'norm_dil': norm_dil, 'w_in_dil': w_in_dil, 'w_out_dil': w_out_dil,
            'norm_moba': norm_moba, 'w_in_moba': w_in_moba, 'w_out_moba': w_out_moba,
            'final_norm': final_norm}


def reference(x_prompt, x_sample, cache_dil0, cache_dil1, cache_dil2, cache_moba, page_table,
              norm_dil, w_in_dil, w_out_dil, norm_moba, w_in_moba, w_out_moba, final_norm):
    dil_caches = (cache_dil0, cache_dil1, cache_dil2)
    S = x_prompt.shape[1]
    pos_p = jnp.arange(S)
    pos_s = page_table.shape[1] * PAGE_SIZE + jnp.arange(x_sample.shape[1])
    xp, xs = x_prompt, x_sample
    dil_new_p = [[] for _ in DIL_GROUPS]
    dil_new_s = [[] for _ in DIL_GROUPS]
    moba_new_p, moba_new_s = [], []
    for i in range(DEPTH):
        j = i // N_MIXERS
        if i % N_MIXERS == 0:
            zp = project(rmsnorm(xp, norm_dil[j]), w_in_dil[j], DIL_COLS)
            zs = project(rmsnorm(xs, norm_dil[j]), w_in_dil[j], DIL_COLS)
            op, lp, osm, lsm = [], [], [], []
            for g, (win, dil) in enumerate(DIL_GROUPS):
                qp = rope(zp[:, :, 3 * g], pos_p)
                kp = rope(zp[:, :, 3 * g + 1], pos_p)
                vp = zp[:, :, 3 * g + 2]
                o, l = dilated_prompt(qp, kp, vp, win, dil)
                op.append(o)
                lp.append(l)
                keep = min(win, S)
                dil_new_p[g].append(jnp.stack([kp[:, S - keep:], vp[:, S - keep:]], axis=2))
                qs = rope(zs[:, :, 3 * g], pos_s)
                ks = rope(zs[:, :, 3 * g + 1], pos_s)
                vs = zs[:, :, 3 * g + 2]
                o, l = dilated_sample(qs, ks, vs, dil_caches[g][j], win, dil)
                osm.append(o)
                lsm.append(l)
                dil_new_s[g].append(jnp.stack([ks, vs], axis=2))
            xp = xp + gate_out(merge_lse(op, lp), zp[:, :, -1], w_out_dil[j])
            xs = xs + gate_out(merge_lse(osm, lsm), zs[:, :, -1], w_out_dil[j])
        else:
            zp = project(rmsnorm(xp, norm_moba[j]), w_in_moba[j], MOBA_COLS)
            zs = project(rmsnorm(xs, norm_moba[j]), w_in_moba[j], MOBA_COLS)
            qp, kp, vp = rope(zp[:, :, 0], pos_p), rope(zp[:, :, 1], pos_p), zp[:, :, 2]
            qs, ks, vs = rope(zs[:, :, 0], pos_s), rope(zs[:, :, 1], pos_s), zs[:, :, 2]
            xp = xp + gate_out(moba_prompt(qp, kp, vp), zp[:, :, 3], w_out_moba[j])
            xs = xs + gate_out(moba_sample(qs, ks, vs, cache_moba[j], page_table), zs[:, :, 3], w_out_moba[j])
            moba_new_p.append(jnp.stack([kp, vp], axis=2))
            moba_new_s.append(jnp.stack([ks, vs], axis=2))
    y_prompt = rmsnorm(xp, final_norm)
    y_sample = rmsnorm(xs, final_norm)
    new_dil0_prompt = jnp.stack(dil_new_p[0])
    new_dil0_sample = jnp.stack(dil_new_s[0])
    new_dil1_prompt = jnp.stack(dil_new_p[1])
    new_dil1_sample = jnp.stack(dil_new_s[1])
    new_dil2_prompt = jnp.stack(dil_new_p[2])
    new_dil2_sample = jnp.stack(dil_new_s[2])
    new_moba_prompt = jnp.stack(moba_new_p)
    new_moba_sample = jnp.stack(moba_new_s)
    return (y_prompt, y_sample, new_dil0_prompt, new_dil0_sample, new_dil1_prompt, new_dil1_sample,
            new_dil2_prompt, new_dil2_sample, new_moba_prompt, new_moba_sample)
```

```python
import functools

import jax
import jax.numpy as jnp
import numpy as np
from jax import lax
from jax.experimental import pallas as pl
from jax.experimental.pallas import tpu as pltpu

HEAD_DIM = 128
DIL_GROUPS = ((128, 1), (512, 4), (2048, 16))
MOBA_BLOCK = 256
MOBA_TOPK = 3
PAGE_SIZE = 128
ROPE_THETA = 10000.0
RMS_EPS = 1e-6
SCALE = HEAD_DIM ** -0.5
NEG = -1e30

LANES = 128
VMEM_LIMIT = 48 * 1024 * 1024

F32 = jnp.float32
BF16 = jnp.bfloat16


def _params(n_axes, vmem=VMEM_LIMIT):
    return pltpu.CompilerParams(dimension_semantics=("arbitrary",) * n_axes, vmem_limit_bytes=vmem)


def _silu(g):
    return g * (1.0 / (1.0 + jnp.exp(-g)))


def _dot_nt(a, b, precision=None):
    return lax.dot_general(a, b, (((1,), (1,)), ((), ())), preferred_element_type=F32, precision=precision)


def _rmsnorm_kernel(x_ref, g_ref, o_ref):
    x = x_ref[...]
    ms = jnp.mean(x * x, axis=-1, keepdims=True)
    o_ref[...] = (x * lax.rsqrt(ms + RMS_EPS) * g_ref[...]).astype(o_ref.dtype)


def _rmsnorm(x, g, out_dtype):
    m, d = x.shape
    tm = min(m, 512)
    return pl.pallas_call(
        _rmsnorm_kernel,
        out_shape=jax.ShapeDtypeStruct((m, d), out_dtype),
        grid=(m // tm,),
        in_specs=[pl.BlockSpec((tm, d), lambda i: (i, 0)), pl.BlockSpec((1, d), lambda i: (0, 0))],
        out_specs=pl.BlockSpec((tm, d), lambda i: (i, 0)),
        compiler_params=_params(1),
        name="rmsnorm",
    )(x, g.reshape(1, d))


def _rope_tables(pos):
    half = HEAD_DIM // 2
    inv = ROPE_THETA ** (-jnp.arange(half, dtype=F32) / half)
    ang = pos.astype(F32)[:, None] * inv[None, :]
    cos, sin = jnp.cos(ang), jnp.sin(ang)
    return jnp.concatenate([cos, cos], axis=-1), jnp.concatenate([-sin, sin], axis=-1)


def _proj_kernel(a_ref, w_ref, cos_ref, sin_ref, o_ref, *, tn, width, rope_cols):
    acc = jnp.dot(a_ref[...].astype(BF16), w_ref[...], preferred_element_type=F32)
    col = (pl.program_id(0) * tn) // width
    is_rope = functools.reduce(jnp.logical_or, [col == c for c in rope_cols])

    @pl.when(is_rope)
    def _():
        cos, sin = cos_ref[...], sin_ref[...]
        for c in range(tn // HEAD_DIM):
            x = acc[:, c * HEAD_DIM:(c + 1) * HEAD_DIM]
            o_ref[:, c * HEAD_DIM:(c + 1) * HEAD_DIM] = x * cos + pltpu.roll(x, HEAD_DIM // 2, 1) * sin

    @pl.when(jnp.logical_not(is_rope))
    def _():
        o_ref[...] = acc


def _project(a, w, cos, sin, rope_cols, width):
    m, k = a.shape
    n = w.shape[1]
    tm = min(m, 512)
    tn = 1024
    pos_blocks = cos.shape[0] // tm
    kern = functools.partial(_proj_kernel, tn=tn, width=width, rope_cols=rope_cols)
    return pl.pallas_call(
        kern,
        out_shape=jax.ShapeDtypeStruct((m, n), F32),
        grid=(n // tn, m // tm),
        in_specs=[pl.BlockSpec((tm, k), lambda j, i: (i, 0)),
                  pl.BlockSpec((k, tn), lambda j, i: (0, j)),
                  pl.BlockSpec((tm, LANES), lambda j, i: (i % pos_blocks, 0)),
                  pl.BlockSpec((tm, LANES), lambda j, i: (i % pos_blocks, 0))],
        out_specs=pl.BlockSpec((tm, tn), lambda j, i: (i, j)),
        compiler_params=_params(2),
        name="proj_rope",
    )(a, w, cos, sin)


def _outproj_kernel(u_ref, w_ref, x_ref, o_ref):
    o_ref[...] = x_ref[...] + jnp.dot(u_ref[...].astype(BF16), w_ref[...], preferred_element_type=F32)


def _out_project(u, w, x):
    m, k = u.shape
    n = w.shape[1]
    tm = min(m, 512)
    tn = 1024
    return pl.pallas_call(
        _outproj_kernel,
        out_shape=jax.ShapeDtypeStruct((m, n), F32),
        grid=(n // tn, m // tm),
        in_specs=[pl.BlockSpec((tm, k), lambda j, i: (i, 0)),
                  pl.BlockSpec((k, tn), lambda j, i: (0, j)),
                  pl.BlockSpec((tm, tn), lambda j, i: (i, j))],
        out_specs=pl.BlockSpec((tm, tn), lambda j, i: (i, j)),
        compiler_params=_params(2),
        name="out_proj",
    )(u, w, x)


def _dil_attn_kernel(q_ref, kp_ref, kc_ref, vp_ref, vc_ref, o_ref, lse_ref, *, n_heads, span):
    has_prev = pl.program_id(2) > 0
    qi = lax.broadcasted_iota(jnp.int32, (span, span), 0)
    ki = lax.broadcasted_iota(jnp.int32, (span, span), 1)
    mask_p = jnp.logical_and(ki >= qi, has_prev)
    mask_c = ki <= qi
    lane = lax.broadcasted_iota(jnp.int32, (span, LANES), 1)
    lse_tile = jnp.zeros((span, LANES), F32)
    for h in range(n_heads):
        sl = slice(h * HEAD_DIM, (h + 1) * HEAD_DIM)
        q = q_ref[:, sl].astype(BF16)
        s_p = jnp.where(mask_p, _dot_nt(q, kp_ref[:, sl].astype(BF16)) * SCALE, NEG)
        s_c = jnp.where(mask_c, _dot_nt(q, kc_ref[:, sl].astype(BF16)) * SCALE, NEG)
        m = jnp.maximum(jnp.max(s_p, axis=1, keepdims=True), jnp.max(s_c, axis=1, keepdims=True))
        p_p = jnp.exp(s_p - m)
        p_c = jnp.exp(s_c - m)
        l = jnp.sum(p_p, axis=1, keepdims=True) + jnp.sum(p_c, axis=1, keepdims=True)
        inv = 1.0 / l
        o = jnp.dot((p_p * inv).astype(BF16), vp_ref[:, sl].astype(BF16), preferred_element_type=F32)
        o = o + jnp.dot((p_c * inv).astype(BF16), vc_ref[:, sl].astype(BF16), preferred_element_type=F32)
        o_ref[:, sl] = o
        lse_tile = jnp.where(lane == h, m + jnp.log(l), lse_tile)
    lse_ref[...] = lse_tile


def _dil_attention(z, g, dil, span, n_cols, width):
    b, s, _ = z.shape
    n_heads = width // HEAD_DIM
    seq = s // dil
    nb = seq // span
    zr = z.reshape(b, seq, dil * n_cols * width)

    def col(c):
        return lambda bi, r, n: (bi, n, r * n_cols + c)

    def col_prev(c):
        return lambda bi, r, n: (bi, jnp.maximum(n - 1, 0), r * n_cols + c)

    blk = (None, span, width)
    kern = functools.partial(_dil_attn_kernel, n_heads=n_heads, span=span)
    o, lse = pl.pallas_call(
        kern,
        out_shape=(jax.ShapeDtypeStruct((b, seq, dil * width), F32),
                   jax.ShapeDtypeStruct((b, seq, dil * LANES), F32)),
        grid=(b, dil, nb),
        in_specs=[pl.BlockSpec(blk, col(3 * g)),
                  pl.BlockSpec(blk, col_prev(3 * g + 1)), pl.BlockSpec(blk, col(3 * g + 1)),
                  pl.BlockSpec(blk, col_prev(3 * g + 2)), pl.BlockSpec(blk, col(3 * g + 2))],
        out_specs=(pl.BlockSpec(blk, lambda bi, r, n: (bi, n, r)),
                   pl.BlockSpec((None, span, LANES), lambda bi, r, n: (bi, n, r))),
        compiler_params=_params(3),
        name=f"dil_attn_d{dil}",
    )(zr, zr, zr, zr, zr)
    return o.reshape(b, s, width), lse.reshape(b, s, LANES)


def _merge_gate_kernel(o0_ref, o1_ref, o2_ref, l0_ref, l1_ref, l2_ref, g_ref, u_ref, *, n_heads):
    o_refs = (o0_ref, o1_ref, o2_ref)
    lses = [r[...] for r in (l0_ref, l1_ref, l2_ref)]
    for h in range(n_heads):
        sl = slice(h * HEAD_DIM, (h + 1) * HEAD_DIM)
        ls = [x[:, h:h + 1] for x in lses]
        mx = jnp.maximum(jnp.maximum(ls[0], ls[1]), ls[2])
        es = [jnp.exp(x - mx) for x in ls]
        inv = 1.0 / (es[0] + es[1] + es[2])
        o = (es[0] * inv) * o_refs[0][:, sl]
        o = o + (es[1] * inv) * o_refs[1][:, sl]
        o = o + (es[2] * inv) * o_refs[2][:, sl]
        u_ref[:, sl] = (o * _silu(g_ref[:, sl])).astype(u_ref.dtype)


def _merge_gate(outs, lses, z2d, gate_col, width):
    m = z2d.shape[0]
    tm = min(m, 256)
    n_heads = width // HEAD_DIM
    row = lambda i: (i, 0)
    kern = functools.partial(_merge_gate_kernel, n_heads=n_heads)
    return pl.pallas_call(
        kern,
        out_shape=jax.ShapeDtypeStruct((m, width), BF16),
        grid=(m // tm,),
        in_specs=[pl.BlockSpec((tm, width), row)] * 3 + [pl.BlockSpec((tm, LANES), row)] * 3
                 + [pl.BlockSpec((tm, width), lambda i: (i, gate_col))],
        out_specs=pl.BlockSpec((tm, width), row),
        compiler_params=_params(1),
        name="dil_merge_gate",
    )(*outs, *lses, z2d)


def _moba_attn_kernel(q_ref, k_ref, v_ref, g_ref, u_ref, kb_ref, vb_ref, kmean_ref, *, hb, nb, blk, topk):
    n = pl.program_id(2)

    @pl.when(n == 0)
    def _():
        kb_ref[...] = k_ref[...].astype(BF16)
        vb_ref[...] = v_ref[...].astype(BF16)
        kmean_ref[...] = jnp.zeros_like(kmean_ref)
        for h in range(hb):
            for j in range(nb):
                kj = k_ref[j * blk:(j + 1) * blk, h * HEAD_DIM:(h + 1) * HEAD_DIM]
                kmean_ref[h, j:j + 1, :] = jnp.sum(kj, axis=0, keepdims=True) * (1.0 / blk)

    qi = lax.broadcasted_iota(jnp.int32, (blk, blk), 0)
    ki = lax.broadcasted_iota(jnp.int32, (blk, blk), 1)
    causal = ki <= qi
    lane = lax.broadcasted_iota(jnp.int32, (blk, LANES), 1)
    own = pl.ds(pl.multiple_of(n * blk, blk), blk)
    for h in range(hb):
        sl = slice(h * HEAD_DIM, (h + 1) * HEAD_DIM)
        qf = q_ref[:, sl]
        qb = qf.astype(BF16)
        gate = _dot_nt(qf, kmean_ref[h], precision=lax.Precision.HIGHEST)
        gate = jnp.where(lane < n, gate, NEG)
        rank = jnp.zeros((blk, LANES), jnp.int32)
        for j in range(nb):
            gj = gate[:, j:j + 1]
            beats = jnp.logical_or(gj > gate, jnp.logical_and(gj == gate, lane > j))
            rank = rank + beats.astype(jnp.int32)
        sel = jnp.where(jnp.logical_and(rank < topk, lane < n), 1.0, 0.0)

        s = jnp.where(causal, _dot_nt(qb, kb_ref[own, sl]) * SCALE, NEG)
        m0 = jnp.max(s, axis=1, keepdims=True)
        p = jnp.exp(s - m0)
        l0 = jnp.sum(p, axis=1, keepdims=True)
        acc0 = jnp.dot(p.astype(BF16), vb_ref[own, sl], preferred_element_type=F32)

        def body(j, carry):
            m, l, acc = carry
            rows = pl.ds(pl.multiple_of(j * blk, blk), blk)
            picked = jnp.max(jnp.where(lane == j, sel, 0.0), axis=1, keepdims=True) > 0.0
            sj = jnp.where(picked, _dot_nt(qb, kb_ref[rows, sl]) * SCALE, NEG)
            m_new = jnp.maximum(m, jnp.max(sj, axis=1, keepdims=True))
            alpha = jnp.exp(m - m_new)
            pj = jnp.exp(sj - m_new)
            l = alpha * l + jnp.sum(pj, axis=1, keepdims=True)
            acc = alpha * acc + jnp.dot(pj.astype(BF16), vb_ref[rows, sl], preferred_element_type=F32)
            return m_new, l, acc

        _, l, acc = lax.fori_loop(0, n, body, (m0, l0, acc0))
        u_ref[:, sl] = (acc * (1.0 / l) * _silu(g_ref[:, sl])).astype(u_ref.dtype)


def _moba_attention(z, width):
    b, s, _ = z.shape
    blk = MOBA_BLOCK
    nb = s // blk
    hb = 2
    cw = hb * HEAD_DIM
    per = width // cw
    topk = min(MOBA_TOPK, nb - 1)
    kern = functools.partial(_moba_attn_kernel, hb=hb, nb=nb, blk=blk, topk=topk)
    return pl.pallas_call(
        kern,
        out_shape=jax.ShapeDtypeStruct((b, s, width), BF16),
        grid=(b, per, nb),
        in_specs=[pl.BlockSpec((None, blk, cw), lambda bi, g, n: (bi, n, g)),
                  pl.BlockSpec((None, s, cw), lambda bi, g, n: (bi, 0, per + g)),
                  pl.BlockSpec((None, s, cw), lambda bi, g, n: (bi, 0, 2 * per + g)),
                  pl.BlockSpec((None, blk, cw), lambda bi, g, n: (bi, n, 3 * per + g))],
        out_specs=pl.BlockSpec((None, blk, cw), lambda bi, g, n: (bi, n, g)),
        scratch_shapes=[pltpu.VMEM((s, cw), BF16), pltpu.VMEM((s, cw), BF16),
                        pltpu.VMEM((hb, LANES, HEAD_DIM), F32)],
        compiler_params=_params(3),
        name="moba_attn",
    )(z, z, z, z)


def _dil_sample_kernel(z_ref, k0_ref, v0_ref, k1_ref, v1_ref, k2_ref, v2_ref, u_ref, *, n_heads, width):
    kv = ((k0_ref, v0_ref), (k1_ref, v1_ref), (k2_ref, v2_ref))
    for h in range(n_heads):
        outs, lses = [], []
        for g in range(len(kv)):
            def zcol(c):
                lo = (3 * g + c) * width + h * HEAD_DIM
                return z_ref[:, lo:lo + HEAD_DIM]
            q, k_new, v_new = zcol(0), zcol(1), zcol(2)
            sl = slice(h * HEAD_DIM, (h + 1) * HEAD_DIM)
            s = jnp.sum(kv[g][0][:, sl] * q, axis=1, keepdims=True) * SCALE
            s_new = jnp.sum(k_new * q, axis=1, keepdims=True) * SCALE
            m = jnp.maximum(jnp.max(s, axis=0, keepdims=True), s_new)
            p = jnp.exp(s - m)
            p_new = jnp.exp(s_new - m)
            l = jnp.sum(p, axis=0, keepdims=True) + p_new
            o = (jnp.sum(p * kv[g][1][:, sl], axis=0, keepdims=True) + p_new * v_new) * (1.0 / l)
            outs.append(o)
            lses.append(m + jnp.log(l))
        mx = jnp.maximum(jnp.maximum(lses[0], lses[1]), lses[2])
        es = [jnp.exp(x - mx) for x in lses]
        inv = 1.0 / (es[0] + es[1] + es[2])
        o = (es[0] * inv) * outs[0] + (es[1] * inv) * outs[1] + (es[2] * inv) * outs[2]
        lo = 9 * width + h * HEAD_DIM
        u_ref[:, h * HEAD_DIM:(h + 1) * HEAD_DIM] = o * _silu(z_ref[:, lo:lo + HEAD_DIM])


def _dil_sample(zs, caches, layer, width):
    db = zs.shape[0]
    n_heads = width // HEAD_DIM
    ins, specs = [zs], [pl.BlockSpec((None, 1, zs.shape[2]), lambda b: (b, 0, 0))]
    for (win, dil), c in zip(DIL_GROUPS, caches):
        span = win // dil
        assert c.shape[2] == win, "cache must hold a full window"
        cr = c.reshape(c.shape[0] * db, span, dil * 2 * width)
        ins += [cr, cr]
        specs += [pl.BlockSpec((None, span, width), lambda b: (layer * db + b, 0, 0)),
                  pl.BlockSpec((None, span, width), lambda b: (layer * db + b, 0, 1))]
    kern = functools.partial(_dil_sample_kernel, n_heads=n_heads, width=width)
    return pl.pallas_call(
        kern,
        out_shape=jax.ShapeDtypeStruct((db, 1, width), F32),
        grid=(db,),
        in_specs=specs,
        out_specs=pl.BlockSpec((None, 1, width), lambda b: (b, 0, 0)),
        compiler_params=_params(1),
        name="dil_sample",
    )(*ins)


def _kmean_kernel(pt_ref, k_ref, o_ref, *, ppb):
    p = pl.program_id(2)

    @pl.when(p == 0)
    def _():
        o_ref[...] = jnp.zeros_like(o_ref)

    row = pl.ds(p // ppb, 1)
    o_ref[row, :] = o_ref[row, :] + jnp.sum(k_ref[...], axis=0, keepdims=True) * (1.0 / (ppb * PAGE_SIZE))


def _pool_view(pool, width):
    return pool.reshape(pool.shape[0] * pool.shape[1], PAGE_SIZE, 2 * width)


def _moba_kmean(pool, layer, page_table, width):
    db, n_pages = page_table.shape
    ppb = MOBA_BLOCK // PAGE_SIZE
    n_full = n_pages // ppb
    rows_blk = 8
    pps = rows_blk * ppb
    base = layer * pool.shape[1]
    kern = functools.partial(_kmean_kernel, ppb=ppb)
    return pl.pallas_call(
        kern,
        out_shape=jax.ShapeDtypeStruct((db, n_full, width), F32),
        grid_spec=pltpu.PrefetchScalarGridSpec(
            num_scalar_prefetch=1,
            grid=(db, n_full // rows_blk, pps),
            in_specs=[pl.BlockSpec((None, PAGE_SIZE, width),
                                   lambda b, t, p, pt: (base + pt[b, t * pps + p], 0, 0))],
            out_specs=pl.BlockSpec((None, rows_blk, width), lambda b, t, p, pt: (b, t, 0)),
        ),
        compiler_params=_params(3),
        name="moba_kmean",
    )(page_table, _pool_view(pool, width))


def _moba_topk_kernel(q_ref, km_ref, o_ref, *, n_heads, topk):
    n_full = km_ref.shape[0]
    idx = lax.broadcasted_iota(jnp.int32, (n_full, 1), 0).astype(F32)
    row = lax.broadcasted_iota(jnp.int32, o_ref.shape, 0)
    lane = lax.broadcasted_iota(jnp.int32, o_ref.shape, 1)
    out = jnp.zeros(o_ref.shape, F32)
    for h in range(n_heads):
        sl = slice(h * HEAD_DIM, (h + 1) * HEAD_DIM)
        gate = jnp.sum(km_ref[:, sl] * q_ref[:, sl], axis=1, keepdims=True)
        for t in range(topk):
            best = jnp.max(gate, axis=0, keepdims=True)
            pick = jnp.min(jnp.where(gate == best, idx, float(n_full)), axis=0, keepdims=True)
            out = jnp.where(jnp.logical_and(row == h, lane == t), pick, out)
            gate = jnp.where(idx == pick, -jnp.inf, gate)
    o_ref[...] = out.astype(jnp.int32)


def _moba_topk(zs, kmean, width, topk):
    db, n_full, _ = kmean.shape
    n_heads = width // HEAD_DIM
    kern = functools.partial(_moba_topk_kernel, n_heads=n_heads, topk=topk)
    return pl.pallas_call(
        kern,
        out_shape=jax.ShapeDtypeStruct((db, n_heads, LANES), jnp.int32),
        grid=(db,),
        in_specs=[pl.BlockSpec((None, 1, width), lambda b: (b, 0, 0)),
                  pl.BlockSpec((None, n_full, width), lambda b: (b, 0, 0))],
        out_specs=pl.BlockSpec((None, n_heads, LANES), lambda b: (b, 0, 0)),
        compiler_params=_params(1),
        name="moba_topk",
    )(zs, kmean)


def _moba_sample_kernel(pt_ref, top_ref, q_ref, kn_ref, vn_ref, g_ref, k_ref, v_ref, u_ref,
                        m_ref, l_ref, acc_ref, *, n_steps):
    t = pl.program_id(2)
    q = q_ref[...]

    @pl.when(t == 0)
    def _():
        m_ref[...] = jnp.sum(kn_ref[...] * q, axis=1, keepdims=True) * SCALE
        l_ref[...] = jnp.ones_like(l_ref)
        acc_ref[...] = vn_ref[...]

    s = jnp.sum(k_ref[...] * q, axis=1, keepdims=True) * SCALE
    m_old = m_ref[...]
    m_new = jnp.maximum(m_old, jnp.max(s, axis=0, keepdims=True))
    alpha = jnp.exp(m_old - m_new)
    p = jnp.exp(s - m_new)
    l_ref[...] = alpha * l_ref[...] + jnp.sum(p, axis=0, keepdims=True)
    acc_ref[...] = alpha * acc_ref[...] + jnp.sum(p * v_ref[...], axis=0, keepdims=True)
    m_ref[...] = m_new

    @pl.when(t == n_steps - 1)
    def _():
        u_ref[...] = acc_ref[...] * (1.0 / l_ref[...]) * _silu(g_ref[...])


def _moba_sample(zs, pool, layer, page_table, top, width, topk):
    db = zs.shape[0]
    n_heads = width // HEAD_DIM
    ppb = MOBA_BLOCK // PAGE_SIZE
    n_steps = topk * ppb
    base = layer * pool.shape[1]
    poolr = _pool_view(pool, width)

    def page(kv):
        def index(b, h, t, pt, tp):
            return (base + pt[b, tp[b, h * topk + t // ppb] * ppb + t % ppb], 0, kv * n_heads + h)
        return index

    def zcol(c):
        return pl.BlockSpec((None, 1, HEAD_DIM), lambda b, h, t, pt, tp: (b, 0, c * n_heads + h))

    kern = functools.partial(_moba_sample_kernel, n_steps=n_steps)
    return pl.pallas_call(
        kern,
        out_shape=jax.ShapeDtypeStruct((db, n_heads, 1, HEAD_DIM), F32),
        grid_spec=pltpu.PrefetchScalarGridSpec(
            num_scalar_prefetch=2,
            grid=(db, n_heads, n_steps),
            in_specs=[zcol(0), zcol(1), zcol(2), zcol(3),
                      pl.BlockSpec((None, PAGE_SIZE, HEAD_DIM), page(0)),
                      pl.BlockSpec((None, PAGE_SIZE, HEAD_DIM), page(1))],
            out_specs=pl.BlockSpec((None, None, 1, HEAD_DIM), lambda b, h, t, pt, tp: (b, h, 0, 0)),
            scratch_shapes=[pltpu.VMEM((1, 1), F32), pltpu.VMEM((1, 1), F32), pltpu.VMEM((1, HEAD_DIM), F32)],
        ),
        compiler_params=_params(3),
        name="moba_sample",
    )(page_table, top, zs, zs, zs, zs, poolr, poolr)


def kernel(x_prompt, x_sample, cache_dil0, cache_dil1, cache_dil2, cache_moba, page_table,
           norm_dil, w_in_dil, w_out_dil, norm_moba, w_in_moba, w_out_moba, final_norm):
    b, s, d = x_prompt.shape
    db, t, _ = x_sample.shape
    width = w_out_dil.shape[1]
    n_heads = width // HEAD_DIM
    depth = norm_dil.shape[0] + norm_moba.shape[0]
    n_pages = page_table.shape[1]
    assert t == 1 and n_pages % (MOBA_BLOCK // PAGE_SIZE) == 0, "decode step: one token, no partial key block"
    dil_caches = (cache_dil0, cache_dil1, cache_dil2)

    cos_p, sin_p = _rope_tables(jnp.arange(s))
    cos_s, sin_s = _rope_tables(jnp.full((db,), n_pages * PAGE_SIZE))
    dil_rope = tuple(c for c in range(9) if c % 3 != 2)
    moba_rope = (0, 1)

    xp = x_prompt.reshape(b * s, d)
    xs = x_sample.reshape(db, d)
    dil_new_p = [[] for _ in DIL_GROUPS]
    dil_new_s = [[] for _ in DIL_GROUPS]
    moba_new_p, moba_new_s = [], []
    for i in range(depth):
        j = i // 2
        if i % 2 == 0:
            w_in = w_in_dil[j].astype(BF16)
            w_out = w_out_dil[j].astype(BF16)
            n_cols = w_in.shape[1] // width
            zp = _project(_rmsnorm(xp, norm_dil[j], BF16), w_in, cos_p, sin_p, dil_rope, width)
            zs = _project(_rmsnorm(xs, norm_dil[j], F32), w_in, cos_s, sin_s, dil_rope, width)
            zp3 = zp.reshape(b, s, n_cols * width)
            outs, lses = [], []
            for g, (win, dil) in enumerate(DIL_GROUPS):
                o, lse = _dil_attention(zp3, g, dil, win // dil, n_cols, width)
                outs.append(o.reshape(b * s, width))
                lses.append(lse.reshape(b * s, LANES))
                keep = min(win, s)
                kv = zp3[:, s - keep:, (3 * g + 1) * width:(3 * g + 3) * width]
                dil_new_p[g].append(kv.reshape(b, keep, 2, n_heads, HEAD_DIM))
                kv_s = zs[:, (3 * g + 1) * width:(3 * g + 3) * width]
                dil_new_s[g].append(kv_s.reshape(db, 1, 2, n_heads, HEAD_DIM))
            up = _merge_gate(outs, lses, zp, n_cols - 1, width)
            us = _dil_sample(zs.reshape(db, 1, n_cols * width), dil_caches, j, width)
            xp = _out_project(up, w_out, xp)
            xs = _out_project(us.reshape(db, width), w_out, xs)
        else:
            w_in = w_in_moba[j].astype(BF16)
            w_out = w_out_moba[j].astype(BF16)
            zp = _project(_rmsnorm(xp, norm_moba[j], BF16), w_in, cos_p, sin_p, moba_rope, width)
            zs = _project(_rmsnorm(xs, norm_moba[j], F32), w_in, cos_s, sin_s, moba_rope, width)
            zp3 = zp.reshape(b, s, 4 * width)
            zs3 = zs.reshape(db, 1, 4 * width)
            up = _moba_attention(zp3, width)
            kmean = _moba_kmean(cache_moba, j, page_table, width)
            topk = min(MOBA_TOPK, kmean.shape[1])
            top = _moba_topk(zs3, kmean, width, topk)[:, :, :topk].reshape(db, n_heads * topk)
            us = _moba_sample(zs3, cache_moba, j, page_table, top, width, topk)
            xp = _out_project(up.reshape(b * s, width), w_out, xp)
            xs = _out_project(us.reshape(db, width), w_out, xs)
            moba_new_p.append(zp3[:, :, width:3 * width].reshape(b, s, 2, n_heads, HEAD_DIM))
            moba_new_s.append(zs3[:, :, width:3 * width].reshape(db, 1, 2, n_heads, HEAD_DIM))
    y_prompt = _rmsnorm(xp, final_norm, F32).reshape(b, s, d)
    y_sample = _rmsnorm(xs, final_norm, F32).reshape(db, 1, d)
    return (y_prompt, y_sample,
            jnp.stack(dil_new_p[0]), jnp.stack(dil_new_s[0]),
            jnp.stack(dil_new_p[1]), jnp.stack(dil_new_s[1]),
            jnp.stack(dil_new_p[2]), jnp.stack(dil_new_s[2]),
            jnp.stack(moba_new_p), jnp.stack(moba_new_s))
```

```python
import functools
import math

import jax
import jax.numpy as jnp
from jax import lax
from jax.experimental import pallas as pl
from jax.experimental.pallas import tpu as pltpu

HEAD_DIM = 128
DIL_GROUPS = ((128, 1), (512, 4), (2048, 16))
MOBA_BLOCK = 256
MOBA_TOPK = 3
PAGE_SIZE = 128
ROPE_THETA = 10000.0
RMS_EPS = 1e-6
SCALE = HEAD_DIM ** -0.5
EXP2_SCALE = SCALE * math.log2(math.e)
NEG = -1e30

LANES = 128
SUBLANES = 8
VMEM_LIMIT = 48 * 1024 * 1024

F32 = jnp.float32
BF16 = jnp.bfloat16


def _params(n_axes, vmem=VMEM_LIMIT):
    return pltpu.CompilerParams(dimension_semantics=("arbitrary",) * n_axes, vmem_limit_bytes=vmem)


def _silu(g):
    return g * (1.0 / (1.0 + jnp.exp(-g)))


def _dot_nn(a, b):
    return jnp.dot(a, b, preferred_element_type=F32)


def _dot_nt(a, b, precision=None):
    return lax.dot_general(a, b, (((1,), (1,)), ((), ())), preferred_element_type=F32, precision=precision)


def _merge3(outs, lses):
    mx = jnp.maximum(jnp.maximum(lses[0], lses[1]), lses[2])
    es = [jnp.exp(x - mx) for x in lses]
    inv = 1.0 / (es[0] + es[1] + es[2])
    return (es[0] * inv) * outs[0] + (es[1] * inv) * outs[1] + (es[2] * inv) * outs[2]


def _rmsnorm_kernel(x_ref, g_ref, o_ref):
    x = x_ref[...]
    ms = jnp.mean(x * x, axis=-1, keepdims=True)
    o_ref[...] = (x * lax.rsqrt(ms + RMS_EPS) * g_ref[...]).astype(o_ref.dtype)


def _rmsnorm(x, g, out_dtype):
    m, d = x.shape
    tm = min(m, 512)
    return pl.pallas_call(
        _rmsnorm_kernel,
        out_shape=jax.ShapeDtypeStruct((m, d), out_dtype),
        grid=(m // tm,),
        in_specs=[pl.BlockSpec((tm, d), lambda i: (i, 0)), pl.BlockSpec((1, d), lambda i: (0, 0))],
        out_specs=pl.BlockSpec((tm, d), lambda i: (i, 0)),
        compiler_params=_params(1),
        name="rmsnorm",
    )(x, g.reshape(1, d))


def _rope_tables(pos):
    half = HEAD_DIM // 2
    inv = ROPE_THETA ** (-jnp.arange(half, dtype=F32) / half)
    ang = pos.astype(F32)[:, None] * inv[None, :]
    cos, sin = jnp.cos(ang), jnp.sin(ang)
    return jnp.concatenate([cos, cos], axis=-1), jnp.concatenate([-sin, sin], axis=-1)


def _proj_kernel(a_ref, w_ref, cos_ref, sin_ref, o_ref, *, tn, width, rope_cols):
    acc = _dot_nn(a_ref[...].astype(BF16), w_ref[...])
    if not rope_cols:
        o_ref[...] = acc
        return
    col = (pl.program_id(0) * tn) // width
    is_rope = functools.reduce(jnp.logical_or, [col == c for c in rope_cols])

    @pl.when(is_rope)
    def _():
        cos, sin = cos_ref[...], sin_ref[...]
        for c in range(tn // HEAD_DIM):
            x = acc[:, c * HEAD_DIM:(c + 1) * HEAD_DIM]
            o_ref[:, c * HEAD_DIM:(c + 1) * HEAD_DIM] = x * cos + pltpu.roll(x, HEAD_DIM // 2, 1) * sin

    @pl.when(jnp.logical_not(is_rope))
    def _():
        o_ref[...] = acc


def _project(a, w, col0, n_out, cos, sin, rope_cols, width):
    m, k = a.shape
    tm = min(m, 512)
    tn = 1024
    off = col0 * width // tn
    pos_blocks = cos.shape[0] // tm
    kern = functools.partial(_proj_kernel, tn=tn, width=width, rope_cols=rope_cols)
    return pl.pallas_call(
        kern,
        out_shape=jax.ShapeDtypeStruct((m, n_out), F32),
        grid=(n_out // tn, m // tm),
        in_specs=[pl.BlockSpec((tm, k), lambda j, i: (i, 0)),
                  pl.BlockSpec((k, tn), lambda j, i: (0, off + j)),
                  pl.BlockSpec((tm, LANES), lambda j, i: (i % pos_blocks, 0)),
                  pl.BlockSpec((tm, LANES), lambda j, i: (i % pos_blocks, 0))],
        out_specs=pl.BlockSpec((tm, tn), lambda j, i: (i, j)),
        compiler_params=_params(2),
        name="proj_rope",
    )(a, w, cos, sin)


def _outproj_kernel(u_ref, w_ref, x_ref, o_ref):
    o_ref[...] = x_ref[...] + _dot_nn(u_ref[...].astype(BF16), w_ref[...])


def _out_project(u, w, x):
    m, k = u.shape
    n = w.shape[1]
    tm = min(m, 512)
    tn = 1024
    return pl.pallas_call(
        _outproj_kernel,
        out_shape=jax.ShapeDtypeStruct((m, n), F32),
        grid=(n // tn, m // tm),
        in_specs=[pl.BlockSpec((tm, k), lambda j, i: (i, 0)),
                  pl.BlockSpec((k, tn), lambda j, i: (0, j)),
                  pl.BlockSpec((tm, tn), lambda j, i: (i, j))],
        out_specs=pl.BlockSpec((tm, tn), lambda j, i: (i, j)),
        compiler_params=_params(2),
        name="out_proj",
    )(u, w, x)


def _dil_attn_kernel(q_ref, k_ref, v_ref, o_ref, lse_ref, kprev_ref, vtprev_ref, lse_s, *, n_heads, span):
    n = pl.program_id(2)

    @pl.when(n == 0)
    def _():
        kprev_ref[...] = jnp.zeros_like(kprev_ref)
        vtprev_ref[...] = jnp.zeros_like(vtprev_ref)
        lse_s[...] = jnp.zeros_like(lse_s)

    key = lax.broadcasted_iota(jnp.int32, (2 * span, span), 0)
    qry = lax.broadcasted_iota(jnp.int32, (2 * span, span), 1)
    prev_ok = jnp.logical_and(jnp.logical_and(key < span, key >= qry), n > 0)
    mask = jnp.logical_or(prev_ok, jnp.logical_and(key >= span, key - span <= qry))
    for h in range(n_heads):
        sl = slice(h * HEAD_DIM, (h + 1) * HEAD_DIM)
        q = q_ref[:, sl].astype(BF16)
        kc = k_ref[:, sl].astype(BF16)
        vtc = v_ref[:, sl].T.astype(BF16)
        s = _dot_nt(jnp.concatenate([kprev_ref[:, sl], kc], axis=0), q)
        s = jnp.where(mask, s, NEG)
        m = jnp.max(s, axis=0, keepdims=True)
        p = jnp.exp2((s - m) * EXP2_SCALE)
        l = jnp.sum(p, axis=0, keepdims=True)
        pn = (p * (1.0 / l)).astype(BF16)
        ot = _dot_nn(jnp.concatenate([vtprev_ref[h], vtc], axis=1), pn)
        o_ref[:, sl] = ot.T
        lse_s[h:h + 1, :] = m * SCALE + jnp.log(l)
        kprev_ref[:, sl] = kc
        vtprev_ref[h] = vtc
    lse_ref[...] = lse_s[...].T


def _dil_attention(z, span, width):
    b, dil, seq, _ = z.shape
    n_heads = width // HEAD_DIM
    assert span == LANES and seq % span == 0
    blk = (None, None, span, width)
    kern = functools.partial(_dil_attn_kernel, n_heads=n_heads, span=span)
    return pl.pallas_call(
        kern,
        out_shape=(jax.ShapeDtypeStruct((b, dil, seq, width), F32),
                   jax.ShapeDtypeStruct((b, dil, seq, LANES), F32)),
        grid=(b, dil, seq // span),
        in_specs=[pl.BlockSpec(blk, lambda bi, r, n: (bi, r, n, 0)),
                  pl.BlockSpec(blk, lambda bi, r, n: (bi, r, n, 1)),
                  pl.BlockSpec(blk, lambda bi, r, n: (bi, r, n, 2))],
        out_specs=(pl.BlockSpec(blk, lambda bi, r, n: (bi, r, n, 0)),
                   pl.BlockSpec((None, None, span, LANES), lambda bi, r, n: (bi, r, n, 0))),
        scratch_shapes=[pltpu.VMEM((span, width), BF16), pltpu.VMEM((n_heads, HEAD_DIM, span), BF16),
                        pltpu.VMEM((LANES, span), F32)],
        compiler_params=_params(3),
        name=f"dil_attn_d{dil}",
    )(z, z, z)


def _merge_gate_kernel(o0_ref, o1_ref, o2_ref, l0_ref, l1_ref, l2_ref, g_ref, u_ref, *, n_heads):
    o_refs = (o0_ref, o1_ref, o2_ref)
    lses = [r[...] for r in (l0_ref, l1_ref, l2_ref)]
    for h in range(n_heads):
        sl = slice(h * HEAD_DIM, (h + 1) * HEAD_DIM)
        o = _merge3([r[:, sl] for r in o_refs], [x[:, h:h + 1] for x in lses])
        u_ref[:, sl] = (o * _silu(g_ref[:, sl])).astype(u_ref.dtype)


def _merge_gate(outs, lses, gate, width):
    m = gate.shape[0]
    tm = min(m, 256)
    n_heads = width // HEAD_DIM
    row = lambda i: (i, 0)
    kern = functools.partial(_merge_gate_kernel, n_heads=n_heads)
    return pl.pallas_call(
        kern,
        out_shape=jax.ShapeDtypeStruct((m, width), BF16),
        grid=(m // tm,),
        in_specs=[pl.BlockSpec((tm, width), row)] * 3 + [pl.BlockSpec((tm, LANES), row)] * 3
                 + [pl.BlockSpec((tm, width), row)],
        out_specs=pl.BlockSpec((tm, width), row),
        compiler_params=_params(1),
        name="dil_merge_gate",
    )(*outs, *lses, gate)


def _moba_attn_kernel(q_ref, k_ref, v_ref, g_ref, u_ref, kb_ref, vt_ref, kmean_ref, sel_ref, acc_ref,
                      *, hb, nb, blk, topk):
    n = pl.program_id(2)
    heads = [slice(h * HEAD_DIM, (h + 1) * HEAD_DIM) for h in range(hb)]

    @pl.when(n == 0)
    def _():
        kb_ref[...] = k_ref[...].astype(BF16)
        kmean_ref[...] = jnp.zeros_like(kmean_ref)
        for h, sl in enumerate(heads):
            for j in range(nb):
                rows = slice(j * blk, (j + 1) * blk)
                vt_ref[h, j // 2, :, (j % 2) * blk:(j % 2 + 1) * blk] = v_ref[rows, sl].T.astype(BF16)
                kmean_ref[h, j:j + 1, :] = jnp.sum(k_ref[rows, sl], axis=0, keepdims=True) * (1.0 / blk)

    nbp = kmean_ref.shape[1]
    blkid = lax.broadcasted_iota(jnp.int32, (nbp, blk), 0)
    causal = lax.broadcasted_iota(jnp.int32, (blk, blk), 0) <= lax.broadcasted_iota(jnp.int32, (blk, blk), 1)
    own = pl.ds(pl.multiple_of(n * blk, blk), blk)
    own_even = n % 2 == 0
    qbs, carry0 = [], []
    for h, sl in enumerate(heads):
        qf = q_ref[:, sl]
        qb = qf.astype(BF16)
        qbs.append(qb)
        gate = _dot_nt(kmean_ref[h], qf, precision=lax.Precision.HIGHEST)
        gate = jnp.where(blkid < n, gate, NEG)
        rank = jnp.zeros((nbp, blk), jnp.int32)
        for j in range(nb):
            gj = gate[j:j + 1, :]
            beats = jnp.logical_or(gj > gate, jnp.logical_and(gj == gate, blkid > j))
            rank = rank + beats.astype(jnp.int32)
        sel_ref[h] = jnp.where(jnp.logical_and(rank < topk, blkid < n), 1.0, 0.0)

        s = jnp.where(causal, _dot_nt(kb_ref[own, sl], qb), NEG)
        m0 = jnp.max(s, axis=0, keepdims=True)
        p = jnp.exp2((s - m0) * EXP2_SCALE)
        p2 = jnp.concatenate([jnp.where(own_even, p, 0.0), jnp.where(own_even, 0.0, p)], axis=0).astype(BF16)
        acc_ref[h] = _dot_nn(vt_ref[h, n // 2], p2)
        carry0.append((m0, jnp.sum(p, axis=0, keepdims=True)))

    def body(i, carry):
        rows = pl.ds(pl.multiple_of(i * (2 * blk), 2 * blk), 2 * blk)
        out = []
        for h, sl in enumerate(heads):
            m, l = carry[h]
            s = _dot_nt(kb_ref[rows, sl], qbs[h])
            sa = jnp.where(sel_ref[h, pl.ds(2 * i, 1), :] > 0.0, s[:blk], NEG)
            sb = jnp.where(sel_ref[h, pl.ds(2 * i + 1, 1), :] > 0.0, s[blk:], NEG)
            m_new = jnp.maximum(m, jnp.maximum(jnp.max(sa, axis=0, keepdims=True), jnp.max(sb, axis=0, keepdims=True)))
            alpha = jnp.exp2((m - m_new) * EXP2_SCALE)
            pa = jnp.exp2((sa - m_new) * EXP2_SCALE)
            pb = jnp.exp2((sb - m_new) * EXP2_SCALE)
            p2 = jnp.concatenate([pa, pb], axis=0).astype(BF16)
            acc_ref[h] = alpha * acc_ref[h] + _dot_nn(vt_ref[h, i], p2)
            out.append((m_new, alpha * l + jnp.sum(pa, axis=0, keepdims=True) + jnp.sum(pb, axis=0, keepdims=True)))
        return tuple(out)

    carry = lax.fori_loop(0, (n + 1) // 2, body, tuple(carry0))
    for h, sl in enumerate(heads):
        o = (acc_ref[h] * (1.0 / carry[h][1])).T
        u_ref[:, sl] = (o * _silu(g_ref[:, sl])).astype(u_ref.dtype)


def _moba_attention(z, width):
    b, s, _ = z.shape
    blk = MOBA_BLOCK
    nb = s // blk
    assert nb % 2 == 0, "key blocks are processed in pairs"
    nbp = -(-nb // SUBLANES) * SUBLANES
    hb = min(4, width // HEAD_DIM)
    cw = hb * HEAD_DIM
    per = width // cw
    topk = min(MOBA_TOPK, nb - 1)
    kern = functools.partial(_moba_attn_kernel, hb=hb, nb=nb, blk=blk, topk=topk)
    resident = functools.partial(pl.BlockSpec, (None, s, cw), pipeline_mode=pl.Buffered(1))
    return pl.pallas_call(
        kern,
        out_shape=jax.ShapeDtypeStruct((b, s, width), BF16),
        grid=(b, per, nb),
        in_specs=[pl.BlockSpec((None, blk, cw), lambda bi, g, n: (bi, n, g)),
                  resident(lambda bi, g, n: (bi, 0, per + g)),
                  resident(lambda bi, g, n: (bi, 0, 2 * per + g)),
                  pl.BlockSpec((None, blk, cw), lambda bi, g, n: (bi, n, 3 * per + g))],
        out_specs=pl.BlockSpec((None, blk, cw), lambda bi, g, n: (bi, n, g)),
        scratch_shapes=[pltpu.VMEM((s, cw), BF16), pltpu.VMEM((hb, nb // 2, HEAD_DIM, 2 * blk), BF16),
                        pltpu.VMEM((hb, nbp, HEAD_DIM), F32), pltpu.VMEM((hb, nbp, blk), F32),
                        pltpu.VMEM((hb, HEAD_DIM, blk), F32)],
        compiler_params=_params(3),
        name="moba_attn",
    )(z, z, z, z)


def _dil_sample_kernel(z_ref, k0_ref, v0_ref, k1_ref, v1_ref, k2_ref, v2_ref, u_ref):
    kv = ((k0_ref, v0_ref), (k1_ref, v1_ref), (k2_ref, v2_ref))
    outs, lses = [], []
    for g, (k_ref, v_ref) in enumerate(kv):
        q, k_new, v_new = z_ref[3 * g], z_ref[3 * g + 1], z_ref[3 * g + 2]
        s = jnp.sum(k_ref[...] * q[None], axis=-1, keepdims=True) * SCALE
        s_new = jnp.sum(k_new * q, axis=-1, keepdims=True) * SCALE
        m = jnp.maximum(jnp.max(s, axis=0), s_new)
        p = jnp.exp(s - m[None])
        p_new = jnp.exp(s_new - m)
        l = jnp.sum(p, axis=0) + p_new
        outs.append((jnp.sum(p * v_ref[...], axis=0) + p_new * v_new) * (1.0 / l))
        lses.append(m + jnp.log(l))
    u_ref[...] = _merge3(outs, lses) * _silu(z_ref[9])


def _dil_sample(zs, caches, layer):
    db, _, n_heads, _ = zs.shape
    ins, specs = [zs], [pl.BlockSpec((None,) + zs.shape[1:], lambda b: (b, 0, 0, 0))]
    for (win, dil), c in zip(DIL_GROUPS, caches):
        span = win // dil
        assert c.shape[2] == win, "cache must hold a full window"
        cr = c.reshape(c.shape[0], db, span, dil, 2, n_heads, HEAD_DIM)
        blk = (None, None, span, None, None, n_heads, HEAD_DIM)
        ins += [cr, cr]
        specs += [pl.BlockSpec(blk, lambda b: (layer, b, 0, 0, 0, 0, 0)),
                  pl.BlockSpec(blk, lambda b: (layer, b, 0, 0, 1, 0, 0))]
    return pl.pallas_call(
        _dil_sample_kernel,
        out_shape=jax.ShapeDtypeStruct((db, n_heads, HEAD_DIM), F32),
        grid=(db,),
        in_specs=specs,
        out_specs=pl.BlockSpec((None, n_heads, HEAD_DIM), lambda b: (b, 0, 0)),
        compiler_params=_params(1),
        name="dil_sample",
    )(*ins)


def _kmean_kernel(pt_ref, *refs, ppb):
    k_refs, o_ref = refs[:-1], refs[-1]
    for i in range(len(k_refs) // ppb):
        acc = jnp.sum(k_refs[i * ppb][...], axis=0)
        for p in range(1, ppb):
            acc = acc + jnp.sum(k_refs[i * ppb + p][...], axis=0)
        o_ref[i] = acc * (1.0 / (ppb * PAGE_SIZE))


def _moba_kmean(pool, layer, page_table):
    db, n_pages = page_table.shape
    n_heads = pool.shape[4]
    ppb = MOBA_BLOCK // PAGE_SIZE
    pps = 8
    assert n_pages % pps == 0 and pps % ppb == 0
    blk = (None, None, PAGE_SIZE, None, n_heads, HEAD_DIM)

    def page(i):
        return lambda b, t, pt: (layer, pt[b, t * pps + i], 0, 0, 0, 0)

    return pl.pallas_call(
        functools.partial(_kmean_kernel, ppb=ppb),
        out_shape=jax.ShapeDtypeStruct((db, n_pages // ppb, n_heads, HEAD_DIM), F32),
        grid_spec=pltpu.PrefetchScalarGridSpec(
            num_scalar_prefetch=1,
            grid=(db, n_pages // pps),
            in_specs=[pl.BlockSpec(blk, page(i)) for i in range(pps)],
            out_specs=pl.BlockSpec((None, pps // ppb, n_heads, HEAD_DIM), lambda b, t, pt: (b, t, 0, 0)),
        ),
        compiler_params=_params(2),
        name="moba_kmean",
    )(page_table, *([pool] * pps))


def _moba_topk_kernel(z_ref, km_ref, o_ref, *, topk):
    n_full = km_ref.shape[0]
    gate = jnp.sum(km_ref[...] * z_ref[0][None], axis=-1, keepdims=True)
    idx = lax.broadcasted_iota(jnp.int32, gate.shape, 0).astype(F32)
    for t in range(topk):
        best = jnp.max(gate, axis=0, keepdims=True)
        pick = jnp.min(jnp.where(gate == best, idx, float(n_full)), axis=0, keepdims=True)
        o_ref[t] = jnp.broadcast_to(pick[0], o_ref.shape[1:]).astype(jnp.int32)
        gate = jnp.where(idx == pick, -jnp.inf, gate)


def _moba_topk(zs, kmean, topk):
    db, n_full, n_heads, _ = kmean.shape
    return pl.pallas_call(
        functools.partial(_moba_topk_kernel, topk=topk),
        out_shape=jax.ShapeDtypeStruct((db, topk, n_heads, LANES), jnp.int32),
        grid=(db,),
        in_specs=[pl.BlockSpec((None,) + zs.shape[1:], lambda b: (b, 0, 0, 0)),
                  pl.BlockSpec((None, n_full, n_heads, HEAD_DIM), lambda b: (b, 0, 0, 0))],
        out_specs=pl.BlockSpec((None, topk, n_heads, LANES), lambda b: (b, 0, 0, 0)),
        compiler_params=_params(1),
        name="moba_topk",
    )(zs, kmean)


def _moba_sample_kernel(pt_ref, top_ref, z_ref, pool_ref, u_ref, kbuf, vbuf, sem,
                        *, layer, n_heads, topk, ppb, n_batch):
    b = pl.program_id(0)
    slot = b % 2

    def copies(bb, sl):
        out = []
        for h in range(n_heads):
            for t in range(topk):
                first = top_ref[bb, h * topk + t] * ppb
                for p in range(ppb):
                    page = pt_ref[bb, first + p]
                    for kvi, buf in enumerate((kbuf, vbuf)):
                        out.append(pltpu.make_async_copy(pool_ref.at[layer, page, :, kvi, h, :],
                                                         buf.at[sl, h, t * ppb + p], sem.at[sl]))
        return out

    @pl.when(b == 0)
    def _():
        for c in copies(0, 0):
            c.start()

    @pl.when(b + 1 < n_batch)
    def _():
        for c in copies(b + 1, 1 - slot):
            c.start()

    for c in copies(b, slot):
        c.wait()

    for h in range(n_heads):
        q, k_new, v_new = (z_ref[c, h:h + 1, :] for c in range(3))
        k = kbuf[slot, h]
        s = jnp.sum(k * q[None], axis=-1, keepdims=True) * SCALE
        s_own = jnp.sum(k_new * q, axis=-1, keepdims=True) * SCALE
        m = jnp.maximum(jnp.max(jnp.max(s, axis=0), axis=0, keepdims=True), s_own)
        p = jnp.exp(s - m[None])
        p_own = jnp.exp(s_own - m)
        l = jnp.sum(jnp.sum(p, axis=0), axis=0, keepdims=True) + p_own
        o = jnp.sum(jnp.sum(p * vbuf[slot, h], axis=0), axis=0, keepdims=True) + p_own * v_new
        u_ref[h:h + 1, :] = o * (1.0 / l) * _silu(z_ref[3, h:h + 1, :])


def _moba_sample(zs, pool, layer, page_table, top, topk):
    db, _, n_heads, _ = zs.shape
    ppb = MOBA_BLOCK // PAGE_SIZE
    kern = functools.partial(_moba_sample_kernel, layer=layer, n_heads=n_heads, topk=topk, ppb=ppb, n_batch=db)
    buf = pltpu.VMEM((2, n_heads, topk * ppb, PAGE_SIZE, HEAD_DIM), F32)
    return pl.pallas_call(
        kern,
        out_shape=jax.ShapeDtypeStruct((db, n_heads, HEAD_DIM), F32),
        grid_spec=pltpu.PrefetchScalarGridSpec(
            num_scalar_prefetch=2,
            grid=(db,),
            in_specs=[pl.BlockSpec((None,) + zs.shape[1:], lambda b, pt, tp: (b, 0, 0, 0)),
                      pl.BlockSpec(memory_space=pl.ANY)],
            out_specs=pl.BlockSpec((None, n_heads, HEAD_DIM), lambda b, pt, tp: (b, 0, 0)),
            scratch_shapes=[buf, buf, pltpu.SemaphoreType.DMA((2,))],
        ),
        compiler_params=_params(1),
        name="moba_sample",
    )(page_table, top, zs, pool)


def kernel(x_prompt, x_sample, cache_dil0, cache_dil1, cache_dil2, cache_moba, page_table,
           norm_dil, w_in_dil, w_out_dil, norm_moba, w_in_moba, w_out_moba, final_norm):
    b, s, d = x_prompt.shape
    db, t, _ = x_sample.shape
    width = w_out_dil.shape[1]
    n_heads = width // HEAD_DIM
    depth = norm_dil.shape[0] + norm_moba.shape[0]
    n_pages = page_table.shape[1]
    assert t == 1 and n_pages % (MOBA_BLOCK // PAGE_SIZE) == 0, "decode step: one token, no partial key block"
    dil_caches = (cache_dil0, cache_dil1, cache_dil2)

    def residue_major(a, dil):
        return jnp.swapaxes(a.reshape((a.shape[0], a.shape[1] // dil, dil) + a.shape[2:]), 1, 2)

    def token_major(a):
        a = jnp.swapaxes(a, 1, 2)
        return a.reshape((a.shape[0] * a.shape[1] * a.shape[2],) + a.shape[3:])

    pos = jnp.arange(s)
    rope_p = [_rope_tables(residue_major(pos[None], dil).reshape(s)) for _, dil in DIL_GROUPS]
    cos_s, sin_s = _rope_tables(jnp.full((db,), n_pages * PAGE_SIZE))
    dil_rope = tuple(c for c in range(9) if c % 3 != 2)

    xp = x_prompt.reshape(b * s, d)
    xs = x_sample.reshape(db, d)
    dil_new_p = [[] for _ in DIL_GROUPS]
    dil_new_s = [[] for _ in DIL_GROUPS]
    moba_new_p, moba_new_s = [], []
    for i in range(depth):
        j = i // 2
        if i % 2 == 0:
            w_in = w_in_dil[j].astype(BF16)
            w_out = w_out_dil[j].astype(BF16)
            n_cols = w_in.shape[1] // width
            hp = _rmsnorm(xp, norm_dil[j], BF16)
            hs = _rmsnorm(xs, norm_dil[j], F32)
            zs = _project(hs, w_in, 0, n_cols * width, cos_s, sin_s, dil_rope, width)
            outs, lses = [], []
            for g, (win, dil) in enumerate(DIL_GROUPS):
                hg = residue_major(hp.reshape(b, s, d), dil).reshape(b * s, d)
                zg = _project(hg, w_in, 3 * g, 3 * width, *rope_p[g], (0, 1), width)
                zg = zg.reshape(b, dil, s // dil, 3 * width)
                o, lse = _dil_attention(zg, win // dil, width)
                outs.append(token_major(o))
                lses.append(token_major(lse))
                keep = min(win, s)
                kv = zg[:, :, (s - keep) // dil:, width:]
                dil_new_p[g].append(token_major(kv).reshape(b, keep, 2, n_heads, HEAD_DIM))
                kv_s = zs[:, (3 * g + 1) * width:(3 * g + 3) * width]
                dil_new_s[g].append(kv_s.reshape(db, 1, 2, n_heads, HEAD_DIM))
            gate = _project(hp, w_in, n_cols - 1, width, *rope_p[0], (), width)
            up = _merge_gate(outs, lses, gate, width)
            us = _dil_sample(zs.reshape(db, n_cols, n_heads, HEAD_DIM), dil_caches, j)
            xp = _out_project(up, w_out, xp)
            xs = _out_project(us.reshape(db, width), w_out, xs)
        else:
            w_in = w_in_moba[j].astype(BF16)
            w_out = w_out_moba[j].astype(BF16)
            cos_p, sin_p = rope_p[0]
            zp = _project(_rmsnorm(xp, norm_moba[j], BF16), w_in, 0, 4 * width, cos_p, sin_p, (0, 1), width)
            zs = _project(_rmsnorm(xs, norm_moba[j], F32), w_in, 0, 4 * width, cos_s, sin_s, (0, 1), width)
            zp3 = zp.reshape(b, s, 4 * width)
            zs4 = zs.reshape(db, 4, n_heads, HEAD_DIM)
            up = _moba_attention(zp3, width)
            kmean = _moba_kmean(cache_moba, j, page_table)
            topk = min(MOBA_TOPK, kmean.shape[1])
            top = _moba_topk(zs4, kmean, topk)[:, :, :, 0]
            top = jnp.swapaxes(top, 1, 2).reshape(db, n_heads * topk)
            us = _moba_sample(zs4, cache_moba, j, page_table, top, topk)
            xp = _out_project(up.reshape(b * s, width), w_out, xp)
            xs = _out_project(us.reshape(db, width), w_out, xs)
            moba_new_p.append(zp3[:, :, width:3 * width].reshape(b, s, 2, n_heads, HEAD_DIM))
            moba_new_s.append(zs[:, width:3 * width].reshape(db, 1, 2, n_heads, HEAD_DIM))
    y_prompt = _rmsnorm(xp, final_norm, F32).reshape(b, s, d)
    y_sample = _rmsnorm(xs, final_norm, F32).reshape(db, 1, d)
    return (y_prompt, y_sample,
            jnp.stack(dil_new_p[0]), jnp.stack(dil_new_s[0]),
            jnp.stack(dil_new_p[1]), jnp.stack(dil_new_s[1]),
            jnp.stack(dil_new_p[2]), jnp.stack(dil_new_s[2]),
            jnp.stack(moba_new_p), jnp.stack(moba_new_s))
```

```python
import functools
import math

import jax
import jax.numpy as jnp
from jax import lax
from jax.experimental import pallas as pl
from jax.experimental.pallas import tpu as pltpu

HEAD_DIM = 128
DIL_GROUPS = ((128, 1), (512, 4), (2048, 16))
MOBA_BLOCK = 256
MOBA_TOPK = 3
PAGE_SIZE = 128
ROPE_THETA = 10000.0
RMS_EPS = 1e-6
SCALE = HEAD_DIM ** -0.5
EXP2_SCALE = SCALE * math.log2(math.e)
NEG = -1e30

LANES = 128
SUBLANES = 8
MXU_COLS = 256
VMEM_LIMIT = 48 * 1024 * 1024

F32 = jnp.float32
BF16 = jnp.bfloat16


def _params(n_axes, vmem=VMEM_LIMIT):
    return pltpu.CompilerParams(dimension_semantics=("arbitrary",) * n_axes, vmem_limit_bytes=vmem)


def _silu(g):
    return g * (1.0 / (1.0 + jnp.exp(-g)))


def _dot_nn(a, b):
    return jnp.dot(a, b, preferred_element_type=F32)


def _dot_nt(a, b, precision=None):
    return lax.dot_general(a, b, (((1,), (1,)), ((), ())), preferred_element_type=F32, precision=precision)


def _merge3(outs, lses):
    mx = jnp.maximum(jnp.maximum(lses[0], lses[1]), lses[2])
    es = [jnp.exp(x - mx) for x in lses]
    inv = 1.0 / (es[0] + es[1] + es[2])
    return (es[0] * inv) * outs[0] + (es[1] * inv) * outs[1] + (es[2] * inv) * outs[2]


def _rmsnorm_kernel(x_ref, g_ref, o_ref):
    x = x_ref[...]
    ms = jnp.mean(x * x, axis=-1, keepdims=True)
    o_ref[...] = (x * lax.rsqrt(ms + RMS_EPS) * g_ref[...]).astype(o_ref.dtype)


def _rmsnorm(x, g, out_dtype):
    m, d = x.shape
    tm = min(m, 512)
    return pl.pallas_call(
        _rmsnorm_kernel,
        out_shape=jax.ShapeDtypeStruct((m, d), out_dtype),
        grid=(m // tm,),
        in_specs=[pl.BlockSpec((tm, d), lambda i: (i, 0)), pl.BlockSpec((1, d), lambda i: (0, 0))],
        out_specs=pl.BlockSpec((tm, d), lambda i: (i, 0)),
        compiler_params=_params(1),
        name="rmsnorm",
    )(x, g.reshape(1, d))


def _rope_tables(pos):
    half = HEAD_DIM // 2
    inv = ROPE_THETA ** (-jnp.arange(half, dtype=F32) / half)
    ang = pos.astype(F32)[:, None] * inv[None, :]
    cos, sin = jnp.cos(ang), jnp.sin(ang)
    return jnp.concatenate([cos, cos], axis=-1), jnp.concatenate([-sin, sin], axis=-1)


def _rope_or_identity(pos):
    cos, sin = _rope_tables(pos)
    return jnp.stack([cos, jnp.ones_like(cos)]), jnp.stack([sin, jnp.zeros_like(sin)])


def _matmul_kernel(a_ref, w_ref, *rest, rope, residual):
    if rope:
        cos_ref, sin_ref, o_ref, wb_ref = rest
    elif residual:
        x_ref, o_ref, wb_ref = rest
    else:
        o_ref, wb_ref = rest

    @pl.when(pl.program_id(1) == 0)
    def _():
        wb_ref[...] = w_ref[...].astype(BF16)

    a = a_ref[...].astype(BF16)
    for c in range(o_ref.shape[1] // MXU_COLS):
        cols = slice(c * MXU_COLS, (c + 1) * MXU_COLS)
        acc = _dot_nn(a, wb_ref[:, cols])
        if rope:
            cos, sin = cos_ref[...], sin_ref[...]
            for lo in range(0, MXU_COLS, HEAD_DIM):
                x = acc[:, lo:lo + HEAD_DIM]
                out = x * cos + pltpu.roll(x, HEAD_DIM // 2, 1) * sin
                o_ref[:, c * MXU_COLS + lo:c * MXU_COLS + lo + HEAD_DIM] = out
        elif residual:
            o_ref[:, cols] = x_ref[:, cols] + acc
        else:
            o_ref[:, cols] = acc


def _matmul(a, w, layer, col0, n_out, *, width, tables=None, rope_cols=(), residual=None, name):
    m, k = a.shape
    tm = min(m, 512)
    tn = 1024
    off = col0 * width // tn
    ins = [a, w]
    specs = [pl.BlockSpec((tm, k), lambda j, i: (i, 0)),
             pl.BlockSpec((None, k, tn), lambda j, i: (layer, 0, off + j))]
    if tables is not None:
        pos_blocks = tables[0].shape[1] // tm

        def table_block(j, i):
            col = (j * tn) // width
            is_rope = sum((col == c).astype(jnp.int32) for c in rope_cols)
            return (1 - is_rope, i % pos_blocks, 0)

        ins += list(tables)
        specs += [pl.BlockSpec((None, tm, LANES), table_block)] * 2
    if residual is not None:
        ins.append(residual)
        specs.append(pl.BlockSpec((tm, tn), lambda j, i: (i, j)))
    kern = functools.partial(_matmul_kernel, rope=tables is not None, residual=residual is not None)
    return pl.pallas_call(
        kern,
        out_shape=jax.ShapeDtypeStruct((m, n_out), F32),
        grid=(n_out // tn, m // tm),
        in_specs=specs,
        out_specs=pl.BlockSpec((tm, tn), lambda j, i: (i, j)),
        scratch_shapes=[pltpu.VMEM((k, tn), BF16)],
        compiler_params=_params(2),
        name=name,
    )(*ins)


def _dil_attn_kernel(q_ref, k_ref, v_ref, o_ref, lse_ref, kprev_ref, vtprev_ref, lse_s, *, n_heads, span):
    n = pl.program_id(2)

    @pl.when(n == 0)
    def _():
        kprev_ref[...] = jnp.zeros_like(kprev_ref)
        vtprev_ref[...] = jnp.zeros_like(vtprev_ref)
        lse_s[...] = jnp.zeros_like(lse_s)

    key = lax.broadcasted_iota(jnp.int32, (2 * span, span), 0)
    qry = lax.broadcasted_iota(jnp.int32, (2 * span, span), 1)
    prev_ok = jnp.logical_and(jnp.logical_and(key < span, key >= qry), n > 0)
    mask = jnp.logical_or(prev_ok, jnp.logical_and(key >= span, key - span <= qry))
    for h in range(n_heads):
        sl = slice(h * HEAD_DIM, (h + 1) * HEAD_DIM)
        q = q_ref[:, sl].astype(BF16)
        kc = k_ref[:, sl].astype(BF16)
        vtc = v_ref[:, sl].T.astype(BF16)
        s = _dot_nt(jnp.concatenate([kprev_ref[:, sl], kc], axis=0), q)
        s = jnp.where(mask, s, NEG)
        m = jnp.max(s, axis=0, keepdims=True)
        p = jnp.exp2((s - m) * EXP2_SCALE)
        l = jnp.sum(p, axis=0, keepdims=True)
        pn = (p * (1.0 / l)).astype(BF16)
        ot = _dot_nn(jnp.concatenate([vtprev_ref[h], vtc], axis=1), pn)
        o_ref[:, sl] = ot.T
        lse_s[h:h + 1, :] = m * SCALE + jnp.log(l)
        kprev_ref[:, sl] = kc
        vtprev_ref[h] = vtc
    lse_ref[...] = lse_s[...].T


def _dil_attention(z, span, width):
    b, dil, seq, _ = z.shape
    n_heads = width // HEAD_DIM
    assert span == LANES and seq % span == 0
    blk = (None, None, span, width)
    kern = functools.partial(_dil_attn_kernel, n_heads=n_heads, span=span)
    return pl.pallas_call(
        kern,
        out_shape=(jax.ShapeDtypeStruct((b, dil, seq, width), F32),
                   jax.ShapeDtypeStruct((b, dil, seq, LANES), F32)),
        grid=(b, dil, seq // span),
        in_specs=[pl.BlockSpec(blk, lambda bi, r, n: (bi, r, n, 0)),
                  pl.BlockSpec(blk, lambda bi, r, n: (bi, r, n, 1)),
                  pl.BlockSpec(blk, lambda bi, r, n: (bi, r, n, 2))],
        out_specs=(pl.BlockSpec(blk, lambda bi, r, n: (bi, r, n, 0)),
                   pl.BlockSpec((None, None, span, LANES), lambda bi, r, n: (bi, r, n, 0))),
        scratch_shapes=[pltpu.VMEM((span, width), BF16), pltpu.VMEM((n_heads, HEAD_DIM, span), BF16),
                        pltpu.VMEM((LANES, span), F32)],
        compiler_params=_params(3),
        name=f"dil_attn_d{dil}",
    )(z, z, z)


def _merge_gate_kernel(o0_ref, o1_ref, o2_ref, l0_ref, l1_ref, l2_ref, g_ref, u_ref, *, n_heads):
    o_refs = (o0_ref, o1_ref, o2_ref)
    lses = [r[...] for r in (l0_ref, l1_ref, l2_ref)]
    for h in range(n_heads):
        sl = slice(h * HEAD_DIM, (h + 1) * HEAD_DIM)
        o = _merge3([r[:, sl] for r in o_refs], [x[:, h:h + 1] for x in lses])
        u_ref[:, sl] = (o * _silu(g_ref[:, sl])).astype(u_ref.dtype)


def _merge_gate(outs, lses, gate, width):
    m = gate.shape[0]
    tm = min(m, 256)
    n_heads = width // HEAD_DIM
    row = lambda i: (i, 0)
    kern = functools.partial(_merge_gate_kernel, n_heads=n_heads)
    return pl.pallas_call(
        kern,
        out_shape=jax.ShapeDtypeStruct((m, width), BF16),
        grid=(m // tm,),
        in_specs=[pl.BlockSpec((tm, width), row)] * 3 + [pl.BlockSpec((tm, LANES), row)] * 3
                 + [pl.BlockSpec((tm, width), row)],
        out_specs=pl.BlockSpec((tm, width), row),
        compiler_params=_params(1),
        name="dil_merge_gate",
    )(*outs, *lses, gate)


def _moba_attn_kernel(q_ref, k_ref, v_ref, g_ref, u_ref, kb_ref, vt_ref, kmean_ref, sel_ref, acc_ref,
                      *, hb, nb, blk, topk):
    n = pl.program_id(2)
    heads = [slice(h * HEAD_DIM, (h + 1) * HEAD_DIM) for h in range(hb)]

    @pl.when(n == 0)
    def _():
        kb_ref[...] = k_ref[...].astype(BF16)
        kmean_ref[...] = jnp.zeros_like(kmean_ref)
        for h, sl in enumerate(heads):
            for j in range(nb):
                rows = slice(j * blk, (j + 1) * blk)
                vt_ref[h, j // 2, :, (j % 2) * blk:(j % 2 + 1) * blk] = v_ref[rows, sl].T.astype(BF16)
                kmean_ref[h, j:j + 1, :] = jnp.sum(k_ref[rows, sl], axis=0, keepdims=True) * (1.0 / blk)

    tq = 2 * blk
    nbp = kmean_ref.shape[1]
    blkid = lax.broadcasted_iota(jnp.int32, (nbp, tq), 0)
    upper = (lax.broadcasted_iota(jnp.int32, (nbp, tq), 1) >= blk).astype(jnp.int32)
    past = blkid < 2 * n + upper
    key = lax.broadcasted_iota(jnp.int32, (tq, tq), 0)
    qry = lax.broadcasted_iota(jnp.int32, (tq, tq), 1)
    causal = jnp.logical_and((key >= blk) == (qry >= blk), key <= qry)
    lower_key = key < blk
    pair = pl.ds(pl.multiple_of(n * tq, tq), tq)
    qbs, carry0 = [], []
    for h, sl in enumerate(heads):
        qf = q_ref[:, sl]
        qb = qf.astype(BF16)
        qbs.append(qb)
        gate = _dot_nt(kmean_ref[h], qf, precision=lax.Precision.HIGHEST)
        gate = jnp.where(past, gate, NEG)
        rank = jnp.zeros((nbp, tq), jnp.int32)
        for j in range(nb):
            gj = gate[j:j + 1, :]
            beats = jnp.logical_or(gj > gate, jnp.logical_and(gj == gate, blkid > j))
            rank = rank + beats.astype(jnp.int32)
        sel_ref[h] = jnp.where(jnp.logical_and(rank < topk, past), 1.0, 0.0)

        picked = sel_ref[h, pl.ds(2 * n, 1), :] > 0.0
        s = _dot_nt(kb_ref[pair, sl], qb)
        s = jnp.where(jnp.logical_or(causal, jnp.logical_and(lower_key, picked)), s, NEG)
        m0 = jnp.max(s, axis=0, keepdims=True)
        p = jnp.exp2((s - m0) * EXP2_SCALE)
        acc_ref[h] = _dot_nn(vt_ref[h, n], p.astype(BF16))
        carry0.append((m0, jnp.sum(p, axis=0, keepdims=True)))

    def body(i, carry):
        rows = pl.ds(pl.multiple_of(i * tq, tq), tq)
        out = []
        for h, sl in enumerate(heads):
            m, l = carry[h]
            s = _dot_nt(kb_ref[rows, sl], qbs[h])
            sa = jnp.where(sel_ref[h, pl.ds(2 * i, 1), :] > 0.0, s[:blk], NEG)
            sb = jnp.where(sel_ref[h, pl.ds(2 * i + 1, 1), :] > 0.0, s[blk:], NEG)
            m_new = jnp.maximum(m, jnp.maximum(jnp.max(sa, axis=0, keepdims=True), jnp.max(sb, axis=0, keepdims=True)))
            alpha = jnp.exp2((m - m_new) * EXP2_SCALE)
            pa = jnp.exp2((sa - m_new) * EXP2_SCALE)
            pb = jnp.exp2((sb - m_new) * EXP2_SCALE)
            p2 = jnp.concatenate([pa, pb], axis=0).astype(BF16)
            acc_ref[h] = alpha * acc_ref[h] + _dot_nn(vt_ref[h, i], p2)
            out.append((m_new, alpha * l + jnp.sum(pa, axis=0, keepdims=True) + jnp.sum(pb, axis=0, keepdims=True)))
        return tuple(out)

    carry = lax.fori_loop(0, n, body, tuple(carry0))
    for h, sl in enumerate(heads):
        o = (acc_ref[h] * (1.0 / carry[h][1])).T
        u_ref[:, sl] = (o * _silu(g_ref[:, sl])).astype(u_ref.dtype)


def _moba_attention(z, width):
    b, s, _ = z.shape
    blk = MOBA_BLOCK
    nb = s // blk
    assert nb % 2 == 0, "query and key blocks are processed in pairs"
    tq = 2 * blk
    nbp = -(-nb // SUBLANES) * SUBLANES
    hb = min(4, width // HEAD_DIM)
    cw = hb * HEAD_DIM
    per = width // cw
    topk = min(MOBA_TOPK, nb - 1)
    kern = functools.partial(_moba_attn_kernel, hb=hb, nb=nb, blk=blk, topk=topk)
    resident = functools.partial(pl.BlockSpec, (None, s, cw), pipeline_mode=pl.Buffered(1))
    return pl.pallas_call(
        kern,
        out_shape=jax.ShapeDtypeStruct((b, s, width), BF16),
        grid=(b, per, nb // 2),
        in_specs=[pl.BlockSpec((None, tq, cw), lambda bi, g, n: (bi, n, g)),
                  resident(lambda bi, g, n: (bi, 0, per + g)),
                  resident(lambda bi, g, n: (bi, 0, 2 * per + g)),
                  pl.BlockSpec((None, tq, cw), lambda bi, g, n: (bi, n, 3 * per + g))],
        out_specs=pl.BlockSpec((None, tq, cw), lambda bi, g, n: (bi, n, g)),
        scratch_shapes=[pltpu.VMEM((s, cw), BF16), pltpu.VMEM((hb, nb // 2, HEAD_DIM, tq), BF16),
                        pltpu.VMEM((hb, nbp, HEAD_DIM), F32), pltpu.VMEM((hb, nbp, tq), F32),
                        pltpu.VMEM((hb, HEAD_DIM, tq), F32)],
        compiler_params=_params(3),
        name="moba_attn",
    )(z, z, z, z)


def _dil_sample_kernel(z_ref, k0_ref, v0_ref, k1_ref, v1_ref, k2_ref, v2_ref, u_ref):
    kv = ((k0_ref, v0_ref), (k1_ref, v1_ref), (k2_ref, v2_ref))
    outs, lses = [], []
    for g, (k_ref, v_ref) in enumerate(kv):
        q, k_new, v_new = z_ref[3 * g], z_ref[3 * g + 1], z_ref[3 * g + 2]
        s = jnp.sum(k_ref[...] * q[None], axis=-1, keepdims=True) * SCALE
        s_new = jnp.sum(k_new * q, axis=-1, keepdims=True) * SCALE
        m = jnp.maximum(jnp.max(s, axis=0), s_new)
        p = jnp.exp(s - m[None])
        p_new = jnp.exp(s_new - m)
        l = jnp.sum(p, axis=0) + p_new
        outs.append((jnp.sum(p * v_ref[...], axis=0) + p_new * v_new) * (1.0 / l))
        lses.append(m + jnp.log(l))
    u_ref[...] = _merge3(outs, lses) * _silu(z_ref[9])


def _dil_sample(zs, caches, layer):
    db, _, n_heads, _ = zs.shape
    ins, specs = [zs], [pl.BlockSpec((None,) + zs.shape[1:], lambda b: (b, 0, 0, 0))]
    for (win, dil), c in zip(DIL_GROUPS, caches):
        span = win // dil
        assert c.shape[2] == win, "cache must hold a full window"
        cr = c.reshape(c.shape[0], db, span, dil, 2, n_heads, HEAD_DIM)
        blk = (None, None, span, None, None, n_heads, HEAD_DIM)
        ins += [cr, cr]
        specs += [pl.BlockSpec(blk, lambda b: (layer, b, 0, 0, 0, 0, 0)),
                  pl.BlockSpec(blk, lambda b: (layer, b, 0, 0, 1, 0, 0))]
    return pl.pallas_call(
        _dil_sample_kernel,
        out_shape=jax.ShapeDtypeStruct((db, n_heads, HEAD_DIM), F32),
        grid=(db,),
        in_specs=specs,
        out_specs=pl.BlockSpec((None, n_heads, HEAD_DIM), lambda b: (b, 0, 0)),
        compiler_params=_params(1),
        name="dil_sample",
    )(*ins)


def _kmean_kernel(pt_ref, *refs, ppb):
    k_refs, o_ref = refs[:-1], refs[-1]
    for i in range(len(k_refs) // ppb):
        acc = jnp.sum(k_refs[i * ppb][...], axis=0)
        for p in range(1, ppb):
            acc = acc + jnp.sum(k_refs[i * ppb + p][...], axis=0)
        o_ref[i] = acc * (1.0 / (ppb * PAGE_SIZE))


def _moba_kmean(pool, layer, page_table):
    db, n_pages = page_table.shape
    n_heads = pool.shape[4]
    ppb = MOBA_BLOCK // PAGE_SIZE
    pps = 8
    assert n_pages % pps == 0 and pps % ppb == 0
    blk = (None, None, PAGE_SIZE, None, n_heads, HEAD_DIM)

    def page(i):
        return lambda b, t, pt: (layer, pt[b, t * pps + i], 0, 0, 0, 0)

    return pl.pallas_call(
        functools.partial(_kmean_kernel, ppb=ppb),
        out_shape=jax.ShapeDtypeStruct((db, n_pages // ppb, n_heads, HEAD_DIM), F32),
        grid_spec=pltpu.PrefetchScalarGridSpec(
            num_scalar_prefetch=1,
            grid=(db, n_pages // pps),
            in_specs=[pl.BlockSpec(blk, page(i)) for i in range(pps)],
            out_specs=pl.BlockSpec((None, pps // ppb, n_heads, HEAD_DIM), lambda b, t, pt: (b, t, 0, 0)),
        ),
        compiler_params=_params(2),
        name="moba_kmean",
    )(page_table, *([pool] * pps))


def _moba_topk_kernel(z_ref, km_ref, o_ref, *, topk):
    n_full = km_ref.shape[0]
    gate = jnp.sum(km_ref[...] * z_ref[0][None], axis=-1, keepdims=True)
    idx = lax.broadcasted_iota(jnp.int32, gate.shape, 0).astype(F32)
    for t in range(topk):
        best = jnp.max(gate, axis=0, keepdims=True)
        pick = jnp.min(jnp.where(gate == best, idx, float(n_full)), axis=0, keepdims=True)
        o_ref[t] = jnp.broadcast_to(pick[0], o_ref.shape[1:]).astype(jnp.int32)
        gate = jnp.where(idx == pick, -jnp.inf, gate)


def _moba_topk(zs, kmean, topk):
    db, n_full, n_heads, _ = kmean.shape
    return pl.pallas_call(
        functools.partial(_moba_topk_kernel, topk=topk),
        out_shape=jax.ShapeDtypeStruct((db, topk, n_heads, LANES), jnp.int32),
        grid=(db,),
        in_specs=[pl.BlockSpec((None,) + zs.shape[1:], lambda b: (b, 0, 0, 0)),
                  pl.BlockSpec((None, n_full, n_heads, HEAD_DIM), lambda b: (b, 0, 0, 0))],
        out_specs=pl.BlockSpec((None, topk, n_heads, LANES), lambda b: (b, 0, 0, 0)),
        compiler_params=_params(1),
        name="moba_topk",
    )(zs, kmean)


def _moba_sample_kernel(pt_ref, top_ref, z_ref, pool_ref, u_ref, kbuf, vbuf, sem,
                        *, layer, n_heads, topk, ppb, n_batch):
    b = pl.program_id(0)
    slot = b % 2

    def copies(bb, sl):
        out = []
        for h in range(n_heads):
            for t in range(topk):
                first = top_ref[bb, h * topk + t] * ppb
                for p in range(ppb):
                    page = pt_ref[bb, first + p]
                    for kvi, buf in enumerate((kbuf, vbuf)):
                        out.append(pltpu.make_async_copy(pool_ref.at[layer, page, :, kvi, h, :],
                                                         buf.at[sl, h, t * ppb + p], sem.at[sl]))
        return out

    @pl.when(b == 0)
    def _():
        for c in copies(0, 0):
            c.start()

    @pl.when(b + 1 < n_batch)
    def _():
        for c in copies(b + 1, 1 - slot):
            c.start()

    for c in copies(b, slot):
        c.wait()

    for h in range(n_heads):
        q, k_new, v_new = (z_ref[c, h:h + 1, :] for c in range(3))
        k = kbuf[slot, h]
        s = jnp.sum(k * q[None], axis=-1, keepdims=True) * SCALE
        s_own = jnp.sum(k_new * q, axis=-1, keepdims=True) * SCALE
        m = jnp.maximum(jnp.max(jnp.max(s, axis=0), axis=0, keepdims=True), s_own)
        p = jnp.exp(s - m[None])
        p_own = jnp.exp(s_own - m)
        l = jnp.sum(jnp.sum(p, axis=0), axis=0, keepdims=True) + p_own
        o = jnp.sum(jnp.sum(p * vbuf[slot, h], axis=0), axis=0, keepdims=True) + p_own * v_new
        u_ref[h:h + 1, :] = o * (1.0 / l) * _silu(z_ref[3, h:h + 1, :])


def _moba_sample(zs, pool, layer, page_table, top, topk):
    db, _, n_heads, _ = zs.shape
    ppb = MOBA_BLOCK // PAGE_SIZE
    kern = functools.partial(_moba_sample_kernel, layer=layer, n_heads=n_heads, topk=topk, ppb=ppb, n_batch=db)
    buf = pltpu.VMEM((2, n_heads, topk * ppb, PAGE_SIZE, HEAD_DIM), F32)
    return pl.pallas_call(
        kern,
        out_shape=jax.ShapeDtypeStruct((db, n_heads, HEAD_DIM), F32),
        grid_spec=pltpu.PrefetchScalarGridSpec(
            num_scalar_prefetch=2,
            grid=(db,),
            in_specs=[pl.BlockSpec((None,) + zs.shape[1:], lambda b, pt, tp: (b, 0, 0, 0)),
                      pl.BlockSpec(memory_space=pl.ANY)],
            out_specs=pl.BlockSpec((None, n_heads, HEAD_DIM), lambda b, pt, tp: (b, 0, 0)),
            scratch_shapes=[buf, buf, pltpu.SemaphoreType.DMA((2,))],
        ),
        compiler_params=_params(1),
        name="moba_sample",
    )(page_table, top, zs, pool)


def kernel(x_prompt, x_sample, cache_dil0, cache_dil1, cache_dil2, cache_moba, page_table,
           norm_dil, w_in_dil, w_out_dil, norm_moba, w_in_moba, w_out_moba, final_norm):
    b, s, d = x_prompt.shape
    db, t, _ = x_sample.shape
    width = w_out_dil.shape[1]
    n_heads = width // HEAD_DIM
    depth = norm_dil.shape[0] + norm_moba.shape[0]
    n_pages = page_table.shape[1]
    assert t == 1 and n_pages % (MOBA_BLOCK // PAGE_SIZE) == 0, "decode step: one token, no partial key block"
    dil_caches = (cache_dil0, cache_dil1, cache_dil2)

    def residue_major(a, dil):
        return jnp.swapaxes(a.reshape((a.shape[0], a.shape[1] // dil, dil) + a.shape[2:]), 1, 2)

    def token_major(a):
        a = jnp.swapaxes(a, 1, 2)
        return a.reshape((a.shape[0] * a.shape[1] * a.shape[2],) + a.shape[3:])

    pos = jnp.arange(s)
    rope_p = [_rope_or_identity(residue_major(pos[None], dil).reshape(s)) for _, dil in DIL_GROUPS]
    rope_s = _rope_or_identity(jnp.full((db,), n_pages * PAGE_SIZE))
    dil_rope = tuple(c for c in range(9) if c % 3 != 2)
    project = functools.partial(_matmul, width=width, name="proj_rope")

    def out_project(u, w, layer, x):
        return _matmul(u, w, layer, 0, w.shape[2], width=width, residual=x, name="out_proj")

    xp = x_prompt.reshape(b * s, d)
    xs = x_sample.reshape(db, d)
    dil_new_p = [[] for _ in DIL_GROUPS]
    dil_new_s = [[] for _ in DIL_GROUPS]
    moba_new_p, moba_new_s = [], []
    for i in range(depth):
        j = i // 2
        if i % 2 == 0:
            w_in, w_out = w_in_dil, w_out_dil
            n_cols = w_in.shape[2] // width
            hp = _rmsnorm(xp, norm_dil[j], BF16)
            hs = _rmsnorm(xs, norm_dil[j], F32)
            zs = project(hs, w_in, j, 0, n_cols * width, tables=rope_s, rope_cols=dil_rope)
            outs, lses = [], []
            for g, (win, dil) in enumerate(DIL_GROUPS):
                hg = residue_major(hp.reshape(b, s, d), dil).reshape(b * s, d)
                zg = project(hg, w_in, j, 3 * g, 3 * width, tables=rope_p[g], rope_cols=(0, 1))
                zg = zg.reshape(b, dil, s // dil, 3 * width)
                o, lse = _dil_attention(zg, win // dil, width)
                outs.append(token_major(o))
                lses.append(token_major(lse))
                keep = min(win, s)
                kv = zg[:, :, (s - keep) // dil:, width:]
                dil_new_p[g].append(token_major(kv).reshape(b, keep, 2, n_heads, HEAD_DIM))
                kv_s = zs[:, (3 * g + 1) * width:(3 * g + 3) * width]
                dil_new_s[g].append(kv_s.reshape(db, 1, 2, n_heads, HEAD_DIM))
            gate = project(hp, w_in, j, n_cols - 1, width)
            up = _merge_gate(outs, lses, gate, width)
            us = _dil_sample(zs.reshape(db, n_cols, n_heads, HEAD_DIM), dil_caches, j)
            xp = out_project(up, w_out, j, xp)
            xs = out_project(us.reshape(db, width), w_out, j, xs)
        else:
            w_in, w_out = w_in_moba, w_out_moba
            zp = project(_rmsnorm(xp, norm_moba[j], BF16), w_in, j, 0, 4 * width, tables=rope_p[0], rope_cols=(0, 1))
            zs = project(_rmsnorm(xs, norm_moba[j], F32), w_in, j, 0, 4 * width, tables=rope_s, rope_cols=(0, 1))
            zp3 = zp.reshape(b, s, 4 * width)
            zs4 = zs.reshape(db, 4, n_heads, HEAD_DIM)
            up = _moba_attention(zp3, width)
            kmean = _moba_kmean(cache_moba, j, page_table)
            topk = min(MOBA_TOPK, kmean.shape[1])
            top = _moba_topk(zs4, kmean, topk)[:, :, :, 0]
            top = jnp.swapaxes(top, 1, 2).reshape(db, n_heads * topk)
            us = _moba_sample(zs4, cache_moba, j, page_table, top, topk)
            xp = out_project(up.reshape(b * s, width), w_out, j, xp)
            xs = out_project(us.reshape(db, width), w_out, j, xs)
            moba_new_p.append(zp3[:, :, width:3 * width].reshape(b, s, 2, n_heads, HEAD_DIM))
            moba_new_s.append(zs[:, width:3 * width].reshape(db, 1, 2, n_heads, HEAD_DIM))
    y_prompt = _rmsnorm(xp, final_norm, F32).reshape(b, s, d)
    y_sample = _rmsnorm(xs, final_norm, F32).reshape(db, 1, d)
    return (y_prompt, y_sample,
            jnp.stack(dil_new_p[0]), jnp.stack(dil_new_s[0]),
            jnp.stack(dil_new_p[1]), jnp.stack(dil_new_s[1]),
            jnp.stack(dil_new_p[2]), jnp.stack(dil_new_s[2]),
            jnp.stack(moba_new_p), jnp.stack(moba_new_s))
```

```python
import functools
import math

import jax
import jax.numpy as jnp
from jax import lax
from jax.experimental import pallas as pl
from jax.experimental.pallas import tpu as pltpu

HEAD_DIM = 128
DIL_GROUPS = ((128, 1), (512, 4), (2048, 16))
MOBA_BLOCK = 256
MOBA_TOPK = 3
PAGE_SIZE = 128
ROPE_THETA = 10000.0
RMS_EPS = 1e-6
SCALE = HEAD_DIM ** -0.5
EXP2_SCALE = SCALE * math.log2(math.e)
NEG = -1e30

LANES = 128
SUBLANES = 8
MXU_COLS = 256
PAGES_PER_STEP = 4
VMEM_LIMIT = 48 * 1024 * 1024

F32 = jnp.float32
BF16 = jnp.bfloat16


def _params(n_axes, vmem=VMEM_LIMIT):
    return pltpu.CompilerParams(dimension_semantics=("arbitrary",) * n_axes, vmem_limit_bytes=vmem)


def _silu(g):
    return g * (1.0 / (1.0 + jnp.exp(-g)))


def _dot_nn(a, b):
    return jnp.dot(a, b, preferred_element_type=F32)


def _dot_nt(a, b, precision=None):
    return lax.dot_general(a, b, (((1,), (1,)), ((), ())), preferred_element_type=F32, precision=precision)


def _merge3(outs, lses):
    mx = jnp.maximum(jnp.maximum(lses[0], lses[1]), lses[2])
    es = [jnp.exp(x - mx) for x in lses]
    inv = 1.0 / (es[0] + es[1] + es[2])
    return (es[0] * inv) * outs[0] + (es[1] * inv) * outs[1] + (es[2] * inv) * outs[2]


def _rmsnorm_kernel(x_ref, g_ref, o_ref):
    x = x_ref[...]
    ms = jnp.mean(x * x, axis=-1, keepdims=True)
    o_ref[...] = (x * lax.rsqrt(ms + RMS_EPS) * g_ref[...]).astype(o_ref.dtype)


def _rmsnorm(x, g, out_dtype):
    m, d = x.shape
    tm = min(m, 512)
    return pl.pallas_call(
        _rmsnorm_kernel,
        out_shape=jax.ShapeDtypeStruct((m, d), out_dtype),
        grid=(m // tm,),
        in_specs=[pl.BlockSpec((tm, d), lambda i: (i, 0)), pl.BlockSpec((1, d), lambda i: (0, 0))],
        out_specs=pl.BlockSpec((tm, d), lambda i: (i, 0)),
        compiler_params=_params(1),
        name="rmsnorm",
    )(x, g.reshape(1, d))


def _rope_tables(pos):
    half = HEAD_DIM // 2
    inv = ROPE_THETA ** (-jnp.arange(half, dtype=F32) / half)
    ang = pos.astype(F32)[:, None] * inv[None, :]
    cos, sin = jnp.cos(ang), jnp.sin(ang)
    return jnp.concatenate([cos, cos], axis=-1), jnp.concatenate([-sin, sin], axis=-1)


def _rope_or_identity(pos):
    cos, sin = _rope_tables(pos)
    return jnp.stack([cos, jnp.ones_like(cos)]), jnp.stack([sin, jnp.zeros_like(sin)])


def _matmul_kernel(*refs, rope, residual, km_steps, grid_steps):
    if km_steps:
        refs = refs[1:]
    a_ref, w_ref, *rest = refs
    wb_ref = rest.pop()
    if km_steps:
        km_ref = rest.pop()
    o_ref = rest.pop()
    if rope:
        cos_ref, sin_ref, *rest = rest
    elif residual:
        x_ref, *rest = rest
    page_refs = rest

    @pl.when(pl.program_id(1) == 0)
    def _():
        wb_ref[...] = w_ref[...].astype(BF16)

    if km_steps:
        ppb = MOBA_BLOCK // PAGE_SIZE

        def block_means():
            for blk in range(len(page_refs) // ppb):
                acc = jnp.sum(page_refs[blk * ppb][...], axis=0)
                for p in range(1, ppb):
                    acc = acc + jnp.sum(page_refs[blk * ppb + p][...], axis=0)
                km_ref[blk] = acc * (1.0 / MOBA_BLOCK)

        if km_steps == grid_steps:
            block_means()
        else:
            pl.when(pl.program_id(0) * pl.num_programs(1) + pl.program_id(1) < km_steps)(block_means)

    a = a_ref[...].astype(BF16)
    for c in range(o_ref.shape[1] // MXU_COLS):
        cols = slice(c * MXU_COLS, (c + 1) * MXU_COLS)
        acc = _dot_nn(a, wb_ref[:, cols])
        if rope:
            cos, sin = cos_ref[...], sin_ref[...]
            for lo in range(0, MXU_COLS, HEAD_DIM):
                x = acc[:, lo:lo + HEAD_DIM]
                out = x * cos + pltpu.roll(x, HEAD_DIM // 2, 1) * sin
                o_ref[:, c * MXU_COLS + lo:c * MXU_COLS + lo + HEAD_DIM] = out
        elif residual:
            o_ref[:, cols] = x_ref[:, cols] + acc
        else:
            o_ref[:, cols] = acc


def _matmul(a, w, layer, col0, n_out, *, width, tail_col=None, tables=None, rope_cols=(), residual=None,
            pages=None, name):
    m, k = a.shape
    tm = min(m, 512)
    tn = 1024
    off = col0 * width // tn
    n_i = m // tm
    grid = (n_out // tn, n_i)
    tail_at = (n_out - width) // tn

    def weight_tile(j, i, *_):
        skip = 0 if tail_col is None else (tail_col - (col0 + n_out // width - 1)) * (width // tn)
        return (layer, 0, off + j + (j >= tail_at).astype(jnp.int32) * skip)

    ins = [a, w]
    specs = [pl.BlockSpec((tm, k), lambda j, i, *_: (i, 0)), pl.BlockSpec((None, k, tn), weight_tile)]
    if tables is not None:
        pos_blocks = tables[0].shape[1] // tm

        def table_block(j, i, *_):
            col = (j * tn) // width
            is_rope = sum((col == c).astype(jnp.int32) for c in rope_cols)
            return (1 - is_rope, i % pos_blocks, 0)

        ins += list(tables)
        specs += [pl.BlockSpec((None, tm, LANES), table_block)] * 2
    if residual is not None:
        ins.append(residual)
        specs.append(pl.BlockSpec((tm, tn), lambda j, i, *_: (i, j)))
    out_shape = jax.ShapeDtypeStruct((m, n_out), F32)
    out_specs = pl.BlockSpec((tm, tn), lambda j, i, *_: (i, j))
    km_steps = 0
    prefetch = []
    if pages is not None:
        pool, pool_layer, page_table, first_step = pages
        db, n_pages = page_table.shape
        n_heads = pool.shape[4]
        steps_per_seq = n_pages // PAGES_PER_STEP
        km_steps = min(grid[0] * grid[1], db * steps_per_seq - first_step)
        blocks_per_step = PAGES_PER_STEP // (MOBA_BLOCK // PAGE_SIZE)

        def page(slot):
            def index(j, i, pt):
                step = first_step + jnp.minimum(j * n_i + i, km_steps - 1)
                return (pool_layer, pt[step // steps_per_seq, (step % steps_per_seq) * PAGES_PER_STEP + slot], 0, 0, 0, 0)
            return index

        prefetch = [page_table]
        ins += [pool] * PAGES_PER_STEP
        specs += [pl.BlockSpec((None, None, PAGE_SIZE, None, n_heads, HEAD_DIM), page(s)) for s in range(PAGES_PER_STEP)]
        out_shape = (out_shape, jax.ShapeDtypeStruct((km_steps * blocks_per_step, n_heads, HEAD_DIM), F32))
        out_specs = (out_specs, pl.BlockSpec((blocks_per_step, n_heads, HEAD_DIM),
                                             lambda j, i, pt: (jnp.minimum(j * n_i + i, km_steps - 1), 0, 0)))
    kern = functools.partial(_matmul_kernel, rope=tables is not None, residual=residual is not None,
                             km_steps=km_steps, grid_steps=grid[0] * grid[1])
    return pl.pallas_call(
        kern,
        out_shape=out_shape,
        grid_spec=pltpu.PrefetchScalarGridSpec(
            num_scalar_prefetch=len(prefetch), grid=grid, in_specs=specs, out_specs=out_specs,
            scratch_shapes=[pltpu.VMEM((k, tn), BF16)]),
        compiler_params=_params(2),
        name=name,
    )(*prefetch, *ins)


def _dil_attn_kernel(q_ref, k_ref, v_ref, o_ref, lse_ref, kprev_ref, vtprev_ref, lse_s, *, n_heads, span):
    n = pl.program_id(2)

    @pl.when(n == 0)
    def _():
        kprev_ref[...] = jnp.zeros_like(kprev_ref)
        vtprev_ref[...] = jnp.zeros_like(vtprev_ref)
        lse_s[...] = jnp.zeros_like(lse_s)

    key = lax.broadcasted_iota(jnp.int32, (2 * span, span), 0)
    qry = lax.broadcasted_iota(jnp.int32, (2 * span, span), 1)
    prev_ok = jnp.logical_and(jnp.logical_and(key < span, key >= qry), n > 0)
    mask = jnp.logical_or(prev_ok, jnp.logical_and(key >= span, key - span <= qry))
    for h in range(n_heads):
        sl = slice(h * HEAD_DIM, (h + 1) * HEAD_DIM)
        q = q_ref[:, sl].astype(BF16)
        kc = k_ref[:, sl].astype(BF16)
        vtc = v_ref[:, sl].T.astype(BF16)
        s = _dot_nt(jnp.concatenate([kprev_ref[:, sl], kc], axis=0), q)
        s = jnp.where(mask, s, NEG)
        m = jnp.max(s, axis=0, keepdims=True)
        p = jnp.exp2((s - m) * EXP2_SCALE)
        l = jnp.sum(p, axis=0, keepdims=True)
        pn = (p * (1.0 / l)).astype(BF16)
        ot = _dot_nn(jnp.concatenate([vtprev_ref[h], vtc], axis=1), pn)
        o_ref[:, sl] = ot.T
        lse_s[h:h + 1, :] = m * SCALE + jnp.log(l)
        kprev_ref[:, sl] = kc
        vtprev_ref[h] = vtc
    lse_ref[...] = lse_s[...].T


def _dil_attention(z, span, width):
    b, dil, seq, _ = z.shape
    n_heads = width // HEAD_DIM
    assert span == LANES and seq % span == 0
    blk = (None, None, span, width)
    kern = functools.partial(_dil_attn_kernel, n_heads=n_heads, span=span)
    return pl.pallas_call(
        kern,
        out_shape=(jax.ShapeDtypeStruct((b, dil, seq, width), F32),
                   jax.ShapeDtypeStruct((b, dil, seq, LANES), F32)),
        grid=(b, dil, seq // span),
        in_specs=[pl.BlockSpec(blk, lambda bi, r, n: (bi, r, n, 0)),
                  pl.BlockSpec(blk, lambda bi, r, n: (bi, r, n, 1)),
                  pl.BlockSpec(blk, lambda bi, r, n: (bi, r, n, 2))],
        out_specs=(pl.BlockSpec(blk, lambda bi, r, n: (bi, r, n, 0)),
                   pl.BlockSpec((None, None, span, LANES), lambda bi, r, n: (bi, r, n, 0))),
        scratch_shapes=[pltpu.VMEM((span, width), BF16), pltpu.VMEM((n_heads, HEAD_DIM, span), BF16),
                        pltpu.VMEM((LANES, span), F32)],
        compiler_params=_params(3),
        name=f"dil_attn_d{dil}",
    )(z, z, z)


def _merge_gate_kernel(o0_ref, o1_ref, o2_ref, l0_ref, l1_ref, l2_ref, g_ref, u_ref, *, n_heads):
    o_refs = (o0_ref, o1_ref, o2_ref)
    lses = [r[...] for r in (l0_ref, l1_ref, l2_ref)]
    for h in range(n_heads):
        sl = slice(h * HEAD_DIM, (h + 1) * HEAD_DIM)
        o = _merge3([r[:, sl] for r in o_refs], [x[:, h:h + 1] for x in lses])
        u_ref[:, sl] = (o * _silu(g_ref[:, sl])).astype(u_ref.dtype)


def _merge_gate(outs, lses, gate, gate_col, width):
    m = gate.shape[0]
    tm = min(m, 256)
    n_heads = width // HEAD_DIM
    row = lambda i: (i, 0)
    kern = functools.partial(_merge_gate_kernel, n_heads=n_heads)
    return pl.pallas_call(
        kern,
        out_shape=jax.ShapeDtypeStruct((m, width), BF16),
        grid=(m // tm,),
        in_specs=[pl.BlockSpec((tm, width), row)] * 3 + [pl.BlockSpec((tm, LANES), row)] * 3
                 + [pl.BlockSpec((tm, width), lambda i: (i, gate_col))],
        out_specs=pl.BlockSpec((tm, width), row),
        compiler_params=_params(1),
        name="dil_merge_gate",
    )(*outs, *lses, gate)


def _moba_attn_kernel(q_ref, k_ref, v_ref, g_ref, u_ref, kb_ref, vt_ref, kmean_ref, sel_ref, acc_ref,
                      *, hb, nb, blk, topk):
    n = pl.program_id(2)
    heads = [slice(h * HEAD_DIM, (h + 1) * HEAD_DIM) for h in range(hb)]

    @pl.when(n == 0)
    def _():
        kb_ref[...] = k_ref[...].astype(BF16)
        kmean_ref[...] = jnp.zeros_like(kmean_ref)
        for h, sl in enumerate(heads):
            for j in range(nb):
                rows = slice(j * blk, (j + 1) * blk)
                vt_ref[h, j // 2, :, (j % 2) * blk:(j % 2 + 1) * blk] = v_ref[rows, sl].T.astype(BF16)
                kmean_ref[h, j:j + 1, :] = jnp.sum(k_ref[rows, sl], axis=0, keepdims=True) * (1.0 / blk)

    tq = 2 * blk
    nbp = kmean_ref.shape[1]
    blkid = lax.broadcasted_iota(jnp.int32, (nbp, tq), 0)
    upper = (lax.broadcasted_iota(jnp.int32, (nbp, tq), 1) >= blk).astype(jnp.int32)
    past = blkid < 2 * n + upper
    key = lax.broadcasted_iota(jnp.int32, (tq, tq), 0)
    qry = lax.broadcasted_iota(jnp.int32, (tq, tq), 1)
    causal = jnp.logical_and((key >= blk) == (qry >= blk), key <= qry)
    lower_key = key < blk
    pair = pl.ds(pl.multiple_of(n * tq, tq), tq)
    qbs, carry0 = [], []
    for h, sl in enumerate(heads):
        qf = q_ref[:, sl]
        qb = qf.astype(BF16)
        qbs.append(qb)
        gate = _dot_nt(kmean_ref[h], qf, precision=lax.Precision.HIGHEST)
        gate = jnp.where(past, gate, NEG)
        rank = jnp.zeros((nbp, tq), jnp.int32)
        for j in range(nb):
            gj = gate[j:j + 1, :]
            beats = jnp.logical_or(gj > gate, jnp.logical_and(gj == gate, blkid > j))
            rank = rank + beats.astype(jnp.int32)
        sel_ref[h] = jnp.where(jnp.logical_and(rank < topk, past), 1.0, 0.0)

        picked = sel_ref[h, pl.ds(2 * n, 1), :] > 0.0
        s = _dot_nt(kb_ref[pair, sl], qb)
        s = jnp.where(jnp.logical_or(causal, jnp.logical_and(lower_key, picked)), s, NEG)
        m0 = jnp.max(s, axis=0, keepdims=True)
        p = jnp.exp2((s - m0) * EXP2_SCALE)
        acc_ref[h] = _dot_nn(vt_ref[h, n], p.astype(BF16))
        carry0.append((m0, jnp.sum(p, axis=0, keepdims=True)))

    def body(i, carry):
        rows = pl.ds(pl.multiple_of(i * tq, tq), tq)
        out = []
        for h, sl in enumerate(heads):
            m, l = carry[h]
            s = _dot_nt(kb_ref[rows, sl], qbs[h])
            sa = jnp.where(sel_ref[h, pl.ds(2 * i, 1), :] > 0.0, s[:blk], NEG)
            sb = jnp.where(sel_ref[h, pl.ds(2 * i + 1, 1), :] > 0.0, s[blk:], NEG)
            m_new = jnp.maximum(m, jnp.maximum(jnp.max(sa, axis=0, keepdims=True), jnp.max(sb, axis=0, keepdims=True)))
            alpha = jnp.exp2((m - m_new) * EXP2_SCALE)
            pa = jnp.exp2((sa - m_new) * EXP2_SCALE)
            pb = jnp.exp2((sb - m_new) * EXP2_SCALE)
            p2 = jnp.concatenate([pa, pb], axis=0).astype(BF16)
            acc_ref[h] = alpha * acc_ref[h] + _dot_nn(vt_ref[h, i], p2)
            out.append((m_new, alpha * l + jnp.sum(pa, axis=0, keepdims=True) + jnp.sum(pb, axis=0, keepdims=True)))
        return tuple(out)

    carry = lax.fori_loop(0, n, body, tuple(carry0))
    for h, sl in enumerate(heads):
        o = (acc_ref[h] * (1.0 / carry[h][1])).T
        u_ref[:, sl] = (o * _silu(g_ref[:, sl])).astype(u_ref.dtype)


def _moba_attention(z, width):
    b, s, _ = z.shape
    blk = MOBA_BLOCK
    nb = s // blk
    assert nb % 2 == 0, "query and key blocks are processed in pairs"
    tq = 2 * blk
    nbp = -(-nb // SUBLANES) * SUBLANES
    hb = min(4, width // HEAD_DIM)
    cw = hb * HEAD_DIM
    per = width // cw
    topk = min(MOBA_TOPK, nb - 1)
    kern = functools.partial(_moba_attn_kernel, hb=hb, nb=nb, blk=blk, topk=topk)
    resident = functools.partial(pl.BlockSpec, (None, s, cw), pipeline_mode=pl.Buffered(1))
    return pl.pallas_call(
        kern,
        out_shape=jax.ShapeDtypeStruct((b, s, width), BF16),
        grid=(b, per, nb // 2),
        in_specs=[pl.BlockSpec((None, tq, cw), lambda bi, g, n: (bi, n, g)),
                  resident(lambda bi, g, n: (bi, 0, per + g)),
                  resident(lambda bi, g, n: (bi, 0, 2 * per + g)),
                  pl.BlockSpec((None, tq, cw), lambda bi, g, n: (bi, n, 3 * per + g))],
        out_specs=pl.BlockSpec((None, tq, cw), lambda bi, g, n: (bi, n, g)),
        scratch_shapes=[pltpu.VMEM((s, cw), BF16), pltpu.VMEM((hb, nb // 2, HEAD_DIM, tq), BF16),
                        pltpu.VMEM((hb, nbp, HEAD_DIM), F32), pltpu.VMEM((hb, nbp, tq), F32),
                        pltpu.VMEM((hb, HEAD_DIM, tq), F32)],
        compiler_params=_params(3),
        name="moba_attn",
    )(z, z, z, z)


def _dil_sample_kernel(z_ref, k0_ref, v0_ref, k1_ref, v1_ref, k2_ref, v2_ref, u_ref):
    kv = ((k0_ref, v0_ref), (k1_ref, v1_ref), (k2_ref, v2_ref))
    outs, lses = [], []
    for g, (k_ref, v_ref) in enumerate(kv):
        q, k_new, v_new = z_ref[3 * g], z_ref[3 * g + 1], z_ref[3 * g + 2]
        s = jnp.sum(k_ref[...] * q[None], axis=-1, keepdims=True) * SCALE
        s_new = jnp.sum(k_new * q, axis=-1, keepdims=True) * SCALE
        m = jnp.maximum(jnp.max(s, axis=0), s_new)
        p = jnp.exp(s - m[None])
        p_new = jnp.exp(s_new - m)
        l = jnp.sum(p, axis=0) + p_new
        outs.append((jnp.sum(p * v_ref[...], axis=0) + p_new * v_new) * (1.0 / l))
        lses.append(m + jnp.log(l))
    u_ref[...] = _merge3(outs, lses) * _silu(z_ref[9])


def _dil_sample(zs, caches, layer):
    db, _, n_heads, _ = zs.shape
    ins, specs = [zs], [pl.BlockSpec((None,) + zs.shape[1:], lambda b: (b, 0, 0, 0))]
    for (win, dil), c in zip(DIL_GROUPS, caches):
        span = win // dil
        assert c.shape[2] == win, "cache must hold a full window"
        cr = c.reshape(c.shape[0], db, span, dil, 2, n_heads, HEAD_DIM)
        blk = (None, None, span, None, None, n_heads, HEAD_DIM)
        ins += [cr, cr]
        specs += [pl.BlockSpec(blk, lambda b: (layer, b, 0, 0, 0, 0, 0)),
                  pl.BlockSpec(blk, lambda b: (layer, b, 0, 0, 1, 0, 0))]
    return pl.pallas_call(
        _dil_sample_kernel,
        out_shape=jax.ShapeDtypeStruct((db, n_heads, HEAD_DIM), F32),
        grid=(db,),
        in_specs=specs,
        out_specs=pl.BlockSpec((None, n_heads, HEAD_DIM), lambda b: (b, 0, 0)),
        compiler_params=_params(1),
        name="dil_sample",
    )(*ins)


def _kmean_kernel(pt_ref, *refs, ppb):
    k_refs, o_ref = refs[:-1], refs[-1]
    for i in range(len(k_refs) // ppb):
        acc = jnp.sum(k_refs[i * ppb][...], axis=0)
        for p in range(1, ppb):
            acc = acc + jnp.sum(k_refs[i * ppb + p][...], axis=0)
        o_ref[i] = acc * (1.0 / (ppb * PAGE_SIZE))


def _moba_kmean(pool, layer, page_table):
    db, n_pages = page_table.shape
    n_heads = pool.shape[4]
    ppb = MOBA_BLOCK // PAGE_SIZE
    pps = 8
    assert n_pages % pps == 0 and pps % ppb == 0
    blk = (None, None, PAGE_SIZE, None, n_heads, HEAD_DIM)

    def page(i):
        return lambda b, t, pt: (layer, pt[b, t * pps + i], 0, 0, 0, 0)

    return pl.pallas_call(
        functools.partial(_kmean_kernel, ppb=ppb),
        out_shape=jax.ShapeDtypeStruct((db, n_pages // ppb, n_heads, HEAD_DIM), F32),
        grid_spec=pltpu.PrefetchScalarGridSpec(
            num_scalar_prefetch=1,
            grid=(db, n_pages // pps),
            in_specs=[pl.BlockSpec(blk, page(i)) for i in range(pps)],
            out_specs=pl.BlockSpec((None, pps // ppb, n_heads, HEAD_DIM), lambda b, t, pt: (b, t, 0, 0)),
        ),
        compiler_params=_params(2),
        name="moba_kmean",
    )(page_table, *([pool] * pps))


def _moba_topk_kernel(z_ref, km_ref, o_ref, *, topk):
    n_full = km_ref.shape[0]
    gate = jnp.sum(km_ref[...] * z_ref[0][None], axis=-1, keepdims=True)
    idx = lax.broadcasted_iota(jnp.int32, gate.shape, 0).astype(F32)
    for t in range(topk):
        best = jnp.max(gate, axis=0, keepdims=True)
        pick = jnp.min(jnp.where(gate == best, idx, float(n_full)), axis=0, keepdims=True)
        o_ref[t] = jnp.broadcast_to(pick[0], o_ref.shape[1:]).astype(jnp.int32)
        gate = jnp.where(idx == pick, -jnp.inf, gate)


def _moba_topk(zs, kmean, topk):
    db, n_full, n_heads, _ = kmean.shape
    return pl.pallas_call(
        functools.partial(_moba_topk_kernel, topk=topk),
        out_shape=jax.ShapeDtypeStruct((db, topk, n_heads, LANES), jnp.int32),
        grid=(db,),
        in_specs=[pl.BlockSpec((None,) + zs.shape[1:], lambda b: (b, 0, 0, 0)),
                  pl.BlockSpec((None, n_full, n_heads, HEAD_DIM), lambda b: (b, 0, 0, 0))],
        out_specs=pl.BlockSpec((None, topk, n_heads, LANES), lambda b: (b, 0, 0, 0)),
        compiler_params=_params(1),
        name="moba_topk",
    )(zs, kmean)


def _moba_sample_kernel(pt_ref, top_ref, z_ref, pool_ref, u_ref, kbuf, vbuf, sem,
                        *, layer, n_heads, topk, ppb, n_batch):
    b = pl.program_id(0)
    slot = b % 2

    def copies(bb, sl):
        out = []
        for h in range(n_heads):
            for t in range(topk):
                first = top_ref[bb, h * topk + t] * ppb
                for p in range(ppb):
                    page = pt_ref[bb, first + p]
                    for kvi, buf in enumerate((kbuf, vbuf)):
                        out.append(pltpu.make_async_copy(pool_ref.at[layer, page, :, kvi, h, :],
                                                         buf.at[sl, h, t * ppb + p], sem.at[sl]))
        return out

    @pl.when(b == 0)
    def _():
        for c in copies(0, 0):
            c.start()

    @pl.when(b + 1 < n_batch)
    def _():
        for c in copies(b + 1, 1 - slot):
            c.start()

    for c in copies(b, slot):
        c.wait()

    for h in range(n_heads):
        q, k_new, v_new = (z_ref[c, h:h + 1, :] for c in range(3))
        k = kbuf[slot, h]
        s = jnp.sum(k * q[None], axis=-1, keepdims=True) * SCALE
        s_own = jnp.sum(k_new * q, axis=-1, keepdims=True) * SCALE
        m = jnp.maximum(jnp.max(jnp.max(s, axis=0), axis=0, keepdims=True), s_own)
        p = jnp.exp(s - m[None])
        p_own = jnp.exp(s_own - m)
        l = jnp.sum(jnp.sum(p, axis=0), axis=0, keepdims=True) + p_own
        o = jnp.sum(jnp.sum(p * vbuf[slot, h], axis=0), axis=0, keepdims=True) + p_own * v_new
        u_ref[h:h + 1, :] = o * (1.0 / l) * _silu(z_ref[3, h:h + 1, :])


def _moba_sample(zs, pool, layer, page_table, top, topk):
    db, _, n_heads, _ = zs.shape
    ppb = MOBA_BLOCK // PAGE_SIZE
    kern = functools.partial(_moba_sample_kernel, layer=layer, n_heads=n_heads, topk=topk, ppb=ppb, n_batch=db)
    buf = pltpu.VMEM((2, n_heads, topk * ppb, PAGE_SIZE, HEAD_DIM), F32)
    return pl.pallas_call(
        kern,
        out_shape=jax.ShapeDtypeStruct((db, n_heads, HEAD_DIM), F32),
        grid_spec=pltpu.PrefetchScalarGridSpec(
            num_scalar_prefetch=2,
            grid=(db,),
            in_specs=[pl.BlockSpec((None,) + zs.shape[1:], lambda b, pt, tp: (b, 0, 0, 0)),
                      pl.BlockSpec(memory_space=pl.ANY)],
            out_specs=pl.BlockSpec((None, n_heads, HEAD_DIM), lambda b, pt, tp: (b, 0, 0)),
            scratch_shapes=[buf, buf, pltpu.SemaphoreType.DMA((2,))],
        ),
        compiler_params=_params(1),
        name="moba_sample",
    )(page_table, top, zs, pool)


def kernel(x_prompt, x_sample, cache_dil0, cache_dil1, cache_dil2, cache_moba, page_table,
           norm_dil, w_in_dil, w_out_dil, norm_moba, w_in_moba, w_out_moba, final_norm):
    b, s, d = x_prompt.shape
    db, t, _ = x_sample.shape
    width = w_out_dil.shape[1]
    n_heads = width // HEAD_DIM
    depth = norm_dil.shape[0] + norm_moba.shape[0]
    n_pages = page_table.shape[1]
    assert t == 1 and n_pages % (MOBA_BLOCK // PAGE_SIZE) == 0, "decode step: one token, no partial key block"
    dil_caches = (cache_dil0, cache_dil1, cache_dil2)

    def residue_major(a, dil):
        return jnp.swapaxes(a.reshape((a.shape[0], a.shape[1] // dil, dil) + a.shape[2:]), 1, 2)

    def token_major(a):
        a = jnp.swapaxes(a, 1, 2)
        return a.reshape((a.shape[0] * a.shape[1] * a.shape[2],) + a.shape[3:])

    pos = jnp.arange(s)
    rope_p = [_rope_or_identity(residue_major(pos[None], dil).reshape(s)) for _, dil in DIL_GROUPS]
    rope_s = _rope_or_identity(jnp.full((db,), n_pages * PAGE_SIZE))
    dil_rope = tuple(c for c in range(9) if c % 3 != 2)
    project = functools.partial(_matmul, width=width, name="proj_rope")

    def out_project(u, w, layer, x):
        return _matmul(u, w, layer, 0, w.shape[2], width=width, residual=x, name="out_proj")

    xp = x_prompt.reshape(b * s, d)
    xs = x_sample.reshape(db, d)
    dil_new_p = [[] for _ in DIL_GROUPS]
    dil_new_s = [[] for _ in DIL_GROUPS]
    moba_new_p, moba_new_s = [], []
    kmeans = {}
    for i in range(depth):
        j = i // 2
        if i % 2 == 0:
            w_in, w_out = w_in_dil, w_out_dil
            n_cols = w_in.shape[2] // width
            hp = _rmsnorm(xp, norm_dil[j], BF16)
            hs = _rmsnorm(xs, norm_dil[j], F32)
            zs = project(hs, w_in, j, 0, n_cols * width, tables=rope_s, rope_cols=dil_rope)
            outs, lses, km_parts, km_done = [], [], [], 0
            km_total = db * n_pages // PAGES_PER_STEP
            for g, (win, dil) in enumerate(DIL_GROUPS):
                hg = residue_major(hp.reshape(b, s, d), dil).reshape(b * s, d)
                pages = (cache_moba, j, page_table, km_done) if j < cache_moba.shape[0] and km_done < km_total else None
                n_grp, tail = (4, n_cols - 1) if dil == 1 else (3, None)
                zg = project(hg, w_in, j, 3 * g, n_grp * width, tail_col=tail, tables=rope_p[g], rope_cols=(0, 1),
                             pages=pages)
                if pages is not None:
                    zg, km = zg
                    km_parts.append(km)
                    km_done += km.shape[0] * (MOBA_BLOCK // PAGE_SIZE) // PAGES_PER_STEP
                if dil == 1:
                    gate = zg
                zg = zg.reshape(b, dil, s // dil, n_grp * width)
                o, lse = _dil_attention(zg, win // dil, width)
                outs.append(token_major(o))
                lses.append(token_major(lse))
                keep = min(win, s)
                kv = zg[:, :, (s - keep) // dil:, width:3 * width]
                dil_new_p[g].append(token_major(kv).reshape(b, keep, 2, n_heads, HEAD_DIM))
                kv_s = zs[:, (3 * g + 1) * width:(3 * g + 3) * width]
                dil_new_s[g].append(kv_s.reshape(db, 1, 2, n_heads, HEAD_DIM))
            up = _merge_gate(outs, lses, gate, 3, width)
            us = _dil_sample(zs.reshape(db, n_cols, n_heads, HEAD_DIM), dil_caches, j)
            xp = out_project(up, w_out, j, xp)
            xs = out_project(us.reshape(db, width), w_out, j, xs)
            if km_parts and km_done == km_total:
                kmeans[j] = jnp.concatenate(km_parts).reshape(db, -1, n_heads, HEAD_DIM)
        else:
            w_in, w_out = w_in_moba, w_out_moba
            zp = project(_rmsnorm(xp, norm_moba[j], BF16), w_in, j, 0, 4 * width, tables=rope_p[0], rope_cols=(0, 1))
            zs = project(_rmsnorm(xs, norm_moba[j], F32), w_in, j, 0, 4 * width, tables=rope_s, rope_cols=(0, 1))
            zp3 = zp.reshape(b, s, 4 * width)
            zs4 = zs.reshape(db, 4, n_heads, HEAD_DIM)
            up = _moba_attention(zp3, width)
            kmean = kmeans[j] if j in kmeans else _moba_kmean(cache_moba, j, page_table)
            topk = min(MOBA_TOPK, kmean.shape[1])
            top = _moba_topk(zs4, kmean, topk)[:, :, :, 0]
            top = jnp.swapaxes(top, 1, 2).reshape(db, n_heads * topk)
            us = _moba_sample(zs4, cache_moba, j, page_table, top, topk)
            xp = out_project(up.reshape(b * s, width), w_out, j, xp)
            xs = out_project(us.reshape(db, width), w_out, j, xs)
            moba_new_p.append(zp3[:, :, width:3 * width].reshape(b, s, 2, n_heads, HEAD_DIM))
            moba_new_s.append(zs[:, width:3 * width].reshape(db, 1, 2, n_heads, HEAD_DIM))
    y_prompt = _rmsnorm(xp, final_norm, F32).reshape(b, s, d)
    y_sample = _rmsnorm(xs, final_norm, F32).reshape(db, 1, d)
    return (y_prompt, y_sample,
            jnp.stack(dil_new_p[0]), jnp.stack(dil_new_s[0]),
            jnp.stack(dil_new_p[1]), jnp.stack(dil_new_s[1]),
            jnp.stack(dil_new_p[2]), jnp.stack(dil_new_s[2]),
            jnp.stack(moba_new_p), jnp.stack(moba_new_s))
```

```python
import functools
import math

import jax
import jax.numpy as jnp
from jax import lax
from jax.experimental import pallas as pl
from jax.experimental.pallas import tpu as pltpu

HEAD_DIM = 128
DIL_GROUPS = ((128, 1), (512, 4), (2048, 16))
MOBA_BLOCK = 256
MOBA_TOPK = 3
PAGE_SIZE = 128
ROPE_THETA = 10000.0
RMS_EPS = 1e-6
SCALE = HEAD_DIM ** -0.5
EXP2_SCALE = SCALE * math.log2(math.e)
NEG = -1e30

LANES = 128
SUBLANES = 8
MXU_COLS = 256
PAGES_PER_STEP = 8
VMEM_LIMIT = 48 * 1024 * 1024
MATMUL_VMEM_LIMIT = 56 * 1024 * 1024

F32 = jnp.float32
BF16 = jnp.bfloat16


def _params(n_axes, vmem=VMEM_LIMIT):
    return pltpu.CompilerParams(dimension_semantics=("arbitrary",) * n_axes, vmem_limit_bytes=vmem)


def _silu(g):
    return g * (1.0 / (1.0 + jnp.exp(-g)))


def _dot_nn(a, b):
    return jnp.dot(a, b, preferred_element_type=F32)


def _dot_nt(a, b, precision=None):
    return lax.dot_general(a, b, (((1,), (1,)), ((), ())), preferred_element_type=F32, precision=precision)


def _merge3(outs, lses):
    mx = jnp.maximum(jnp.maximum(lses[0], lses[1]), lses[2])
    es = [jnp.exp(x - mx) for x in lses]
    inv = 1.0 / (es[0] + es[1] + es[2])
    return (es[0] * inv) * outs[0] + (es[1] * inv) * outs[1] + (es[2] * inv) * outs[2]


def _rmsnorm_kernel(x_ref, g_ref, o_ref):
    x = x_ref[...]
    ms = jnp.mean(x * x, axis=-1, keepdims=True)
    o_ref[...] = (x * lax.rsqrt(ms + RMS_EPS) * g_ref[...]).astype(o_ref.dtype)


def _rmsnorm(x, g, out_dtype):
    m, d = x.shape
    tm = min(m, 512)
    return pl.pallas_call(
        _rmsnorm_kernel,
        out_shape=jax.ShapeDtypeStruct((m, d), out_dtype),
        grid=(m // tm,),
        in_specs=[pl.BlockSpec((tm, d), lambda i: (i, 0)), pl.BlockSpec((1, d), lambda i: (0, 0))],
        out_specs=pl.BlockSpec((tm, d), lambda i: (i, 0)),
        compiler_params=_params(1),
        name="rmsnorm",
    )(x, g.reshape(1, d))


def _rope_tables(pos):
    half = HEAD_DIM // 2
    inv = ROPE_THETA ** (-jnp.arange(half, dtype=F32) / half)
    ang = pos.astype(F32)[:, None] * inv[None, :]
    cos, sin = jnp.cos(ang), jnp.sin(ang)
    return jnp.concatenate([cos, cos], axis=-1), jnp.concatenate([-sin, sin], axis=-1)


def _rope_or_identity(pos):
    cos, sin = _rope_tables(pos)
    rot = jnp.concatenate([cos, sin], axis=-1)
    keep = jnp.concatenate([jnp.ones_like(cos), jnp.zeros_like(sin)], axis=-1)
    return jnp.stack([rot, keep])


def _matmul_kernel(*refs, rope, residual, km_steps, grid_steps):
    if km_steps:
        refs = refs[1:]
    a_ref, w_ref, *rest = refs
    wb_ref = rest.pop()
    if km_steps:
        km_ref = rest.pop()
    o_ref = rest.pop()
    if rope:
        table_ref, *rest = rest
    elif residual:
        x_ref, *rest = rest
    page_refs = rest

    @pl.when(pl.program_id(1) == 0)
    def _():
        wb_ref[...] = w_ref[...].astype(BF16)

    if km_steps:
        ppb = MOBA_BLOCK // PAGE_SIZE

        def block_means():
            for blk in range(len(page_refs) // ppb):
                acc = jnp.sum(page_refs[blk * ppb][...], axis=0)
                for p in range(1, ppb):
                    acc = acc + jnp.sum(page_refs[blk * ppb + p][...], axis=0)
                km_ref[blk] = acc * (1.0 / MOBA_BLOCK)

        if km_steps == grid_steps:
            block_means()
        else:
            pl.when(pl.program_id(0) * pl.num_programs(1) + pl.program_id(1) < km_steps)(block_means)

    a = a_ref[...].astype(BF16)
    for c in range(o_ref.shape[1] // MXU_COLS):
        cols = slice(c * MXU_COLS, (c + 1) * MXU_COLS)
        acc = _dot_nn(a, wb_ref[:, cols])
        if rope:
            cos, sin = table_ref[:, :LANES], table_ref[:, LANES:]
            for lo in range(0, MXU_COLS, HEAD_DIM):
                x = acc[:, lo:lo + HEAD_DIM]
                out = x * cos + pltpu.roll(x, HEAD_DIM // 2, 1) * sin
                o_ref[:, c * MXU_COLS + lo:c * MXU_COLS + lo + HEAD_DIM] = out
        elif residual:
            o_ref[:, cols] = x_ref[:, cols] + acc
        else:
            o_ref[:, cols] = acc


def _matmul(a, w, layer, col0, n_out, *, width, tail_col=None, tables=None, rope_cols=(), residual=None,
            pages=None, name):
    m, k = a.shape
    tm = min(m, 1024)
    tn = 1024
    off = col0 * width // tn
    n_i = m // tm
    grid = (n_out // tn, n_i)
    tail_at = (n_out - width) // tn

    def weight_tile(j, i, *_):
        skip = 0 if tail_col is None else (tail_col - (col0 + n_out // width - 1)) * (width // tn)
        return (layer, 0, off + j + (j >= tail_at).astype(jnp.int32) * skip)

    ins = [a, w]
    specs = [pl.BlockSpec((tm, k), lambda j, i, *_: (i, 0)),
             pl.BlockSpec((None, k, tn), weight_tile, pipeline_mode=pl.Buffered(1))]
    if tables is not None:
        assert tables.shape[1] % tm == 0, "row tiles must not straddle sequences"
        pos_blocks = tables.shape[1] // tm

        def table_block(j, i, *_):
            col = (j * tn) // width
            is_rope = sum((col == c).astype(jnp.int32) for c in rope_cols)
            return (1 - is_rope, i % pos_blocks, 0)

        ins.append(tables)
        specs.append(pl.BlockSpec((None, tm, 2 * LANES), table_block))
    if residual is not None:
        ins.append(residual)
        specs.append(pl.BlockSpec((tm, tn), lambda j, i, *_: (i, j)))
    out_shape = jax.ShapeDtypeStruct((m, n_out), F32)
    out_specs = pl.BlockSpec((tm, tn), lambda j, i, *_: (i, j))
    km_steps = 0
    prefetch = []
    if pages is not None:
        pool, pool_layer, page_table, first_step = pages
        db, n_pages = page_table.shape
        n_heads = pool.shape[4]
        steps_per_seq = n_pages // PAGES_PER_STEP
        km_steps = min(grid[0] * grid[1], db * steps_per_seq - first_step)
        blocks_per_step = PAGES_PER_STEP // (MOBA_BLOCK // PAGE_SIZE)

        def page(slot):
            def index(j, i, pt):
                step = first_step + jnp.minimum(j * n_i + i, km_steps - 1)
                return (pool_layer, pt[step // steps_per_seq, (step % steps_per_seq) * PAGES_PER_STEP + slot], 0, 0, 0, 0)
            return index

        prefetch = [page_table]
        ins += [pool] * PAGES_PER_STEP
        specs += [pl.BlockSpec((None, None, PAGE_SIZE, None, n_heads, HEAD_DIM), page(s)) for s in range(PAGES_PER_STEP)]
        out_shape = (out_shape, jax.ShapeDtypeStruct((km_steps * blocks_per_step, n_heads, HEAD_DIM), F32))
        out_specs = (out_specs, pl.BlockSpec((blocks_per_step, n_heads, HEAD_DIM),
                                             lambda j, i, pt: (jnp.minimum(j * n_i + i, km_steps - 1), 0, 0)))
    kern = functools.partial(_matmul_kernel, rope=tables is not None, residual=residual is not None,
                             km_steps=km_steps, grid_steps=grid[0] * grid[1])
    return pl.pallas_call(
        kern,
        out_shape=out_shape,
        grid_spec=pltpu.PrefetchScalarGridSpec(
            num_scalar_prefetch=len(prefetch), grid=grid, in_specs=specs, out_specs=out_specs,
            scratch_shapes=[pltpu.VMEM((k, tn), BF16)]),
        compiler_params=_params(2, MATMUL_VMEM_LIMIT),
        name=name,
    )(*prefetch, *ins)


def _dil_attn_kernel(z_ref, o_ref, kprev_ref, vtprev_ref, lse_s, *, n_heads, span):
    n = pl.program_id(2)
    width = n_heads * HEAD_DIM

    @pl.when(n == 0)
    def _():
        kprev_ref[...] = jnp.zeros_like(kprev_ref)
        vtprev_ref[...] = jnp.zeros_like(vtprev_ref)
        lse_s[...] = jnp.zeros_like(lse_s)

    key = lax.broadcasted_iota(jnp.int32, (2 * span, span), 0)
    qry = lax.broadcasted_iota(jnp.int32, (2 * span, span), 1)
    prev_ok = jnp.logical_and(jnp.logical_and(key < span, key >= qry), n > 0)
    mask = jnp.logical_or(prev_ok, jnp.logical_and(key >= span, key - span <= qry))
    for h in range(n_heads):
        sl = slice(h * HEAD_DIM, (h + 1) * HEAD_DIM)
        q = z_ref[:, sl].astype(BF16)
        kc = z_ref[:, width + h * HEAD_DIM:width + (h + 1) * HEAD_DIM].astype(BF16)
        vtc = z_ref[:, 2 * width + h * HEAD_DIM:2 * width + (h + 1) * HEAD_DIM].T.astype(BF16)
        s = _dot_nt(jnp.concatenate([kprev_ref[:, sl], kc], axis=0), q)
        s = jnp.where(mask, s, NEG)
        m = jnp.max(s, axis=0, keepdims=True)
        p = jnp.exp2((s - m) * EXP2_SCALE)
        l = jnp.sum(p, axis=0, keepdims=True)
        pn = (p * (1.0 / l)).astype(BF16)
        ot = _dot_nn(jnp.concatenate([vtprev_ref[h], vtc], axis=1), pn)
        o_ref[:, sl] = ot.T
        lse_s[h:h + 1, :] = m * SCALE + jnp.log(l)
        kprev_ref[:, sl] = kc
        vtprev_ref[h] = vtc
    o_ref[:, width:] = lse_s[...].T


def _dil_attention(z, span, width):
    b, dil, seq, _ = z.shape
    n_heads = width // HEAD_DIM
    assert span == LANES and seq % span == 0
    kern = functools.partial(_dil_attn_kernel, n_heads=n_heads, span=span)
    return pl.pallas_call(
        kern,
        out_shape=jax.ShapeDtypeStruct((b, dil, seq, width + LANES), F32),
        grid=(b, dil, seq // span),
        in_specs=[pl.BlockSpec((None, None, span, 3 * width), lambda bi, r, n: (bi, r, n, 0))],
        out_specs=pl.BlockSpec((None, None, span, width + LANES), lambda bi, r, n: (bi, r, n, 0)),
        scratch_shapes=[pltpu.VMEM((span, width), BF16), pltpu.VMEM((n_heads, HEAD_DIM, span), BF16),
                        pltpu.VMEM((LANES, span), F32)],
        compiler_params=_params(3),
        name=f"dil_attn_d{dil}",
    )(z)


def _merge_gate_kernel(o0_ref, o1_ref, o2_ref, g_ref, u_ref, *, n_heads):
    o_refs = (o0_ref, o1_ref, o2_ref)
    width = n_heads * HEAD_DIM
    lses = [r[:, width:] for r in o_refs]
    for h in range(n_heads):
        sl = slice(h * HEAD_DIM, (h + 1) * HEAD_DIM)
        o = _merge3([r[:, sl] for r in o_refs], [x[:, h:h + 1] for x in lses])
        u_ref[:, sl] = (o * _silu(g_ref[:, sl])).astype(u_ref.dtype)


def _merge_gate(outs, gate, gate_col, width):
    m = gate.shape[0]
    tm = min(m, 256)
    n_heads = width // HEAD_DIM
    row = lambda i: (i, 0)
    kern = functools.partial(_merge_gate_kernel, n_heads=n_heads)
    return pl.pallas_call(
        kern,
        out_shape=jax.ShapeDtypeStruct((m, width), BF16),
        grid=(m // tm,),
        in_specs=[pl.BlockSpec((tm, width + LANES), row)] * 3 + [pl.BlockSpec((tm, width), lambda i: (i, gate_col))],
        out_specs=pl.BlockSpec((tm, width), row),
        compiler_params=_params(1),
        name="dil_merge_gate",
    )(*outs, gate)


def _moba_attn_kernel(q_ref, k_ref, v_ref, g_ref, u_ref, kb_ref, vt_ref, kmean_ref, sel_ref, acc_ref,
                      *, hb, nb, blk, topk):
    n = pl.program_id(2)
    heads = [slice(h * HEAD_DIM, (h + 1) * HEAD_DIM) for h in range(hb)]

    @pl.when(n == 0)
    def _():
        kb_ref[...] = k_ref[...].astype(BF16)
        kmean_ref[...] = jnp.zeros_like(kmean_ref)
        for h, sl in enumerate(heads):
            for j in range(nb):
                rows = slice(j * blk, (j + 1) * blk)
                vt_ref[h, j // 2, :, (j % 2) * blk:(j % 2 + 1) * blk] = v_ref[rows, sl].T.astype(BF16)
                kmean_ref[h, j:j + 1, :] = jnp.sum(k_ref[rows, sl], axis=0, keepdims=True) * (1.0 / blk)

    tq = 2 * blk
    nbp = kmean_ref.shape[1]
    blkid = lax.broadcasted_iota(jnp.int32, (nbp, tq), 0)
    upper = (lax.broadcasted_iota(jnp.int32, (nbp, tq), 1) >= blk).astype(jnp.int32)
    past = blkid < 2 * n + upper
    key = lax.broadcasted_iota(jnp.int32, (tq, tq), 0)
    qry = lax.broadcasted_iota(jnp.int32, (tq, tq), 1)
    causal = jnp.logical_and((key >= blk) == (qry >= blk), key <= qry)
    lower_key = key < blk
    pair = pl.ds(pl.multiple_of(n * tq, tq), tq)
    qbs, carry0 = [], []
    for h, sl in enumerate(heads):
        qf = q_ref[:, sl]
        qb = qf.astype(BF16)
        qbs.append(qb)
        gate = _dot_nt(kmean_ref[h], qf, precision=lax.Precision.HIGHEST)
        gate = jnp.where(past, gate, NEG)
        rank = jnp.zeros((nbp, tq), jnp.int32)
        for j in range(nb):
            gj = gate[j:j + 1, :]
            beats = jnp.logical_or(gj > gate, jnp.logical_and(gj == gate, blkid > j))
            rank = rank + beats.astype(jnp.int32)
        sel_ref[h] = jnp.where(jnp.logical_and(rank < topk, past), 1.0, 0.0)

        picked = sel_ref[h, pl.ds(2 * n, 1), :] > 0.0
        s = _dot_nt(kb_ref[pair, sl], qb)
        s = jnp.where(jnp.logical_or(causal, jnp.logical_and(lower_key, picked)), s, NEG)
        m0 = jnp.max(s, axis=0, keepdims=True)
        p = jnp.exp2((s - m0) * EXP2_SCALE)
        acc_ref[h] = _dot_nn(vt_ref[h, n], p.astype(BF16))
        carry0.append((m0, jnp.sum(p, axis=0, keepdims=True)))

    def body(i, carry):
        rows = pl.ds(pl.multiple_of(i * tq, tq), tq)
        out = []
        for h, sl in enumerate(heads):
            m, l = carry[h]
            s = _dot_nt(kb_ref[rows, sl], qbs[h])
            sa = jnp.where(sel_ref[h, pl.ds(2 * i, 1), :] > 0.0, s[:blk], NEG)
            sb = jnp.where(sel_ref[h, pl.ds(2 * i + 1, 1), :] > 0.0, s[blk:], NEG)
            m_new = jnp.maximum(m, jnp.maximum(jnp.max(sa, axis=0, keepdims=True), jnp.max(sb, axis=0, keepdims=True)))
            alpha = jnp.exp2((m - m_new) * EXP2_SCALE)
            pa = jnp.exp2((sa - m_new) * EXP2_SCALE)
            pb = jnp.exp2((sb - m_new) * EXP2_SCALE)
            p2 = jnp.concatenate([pa, pb], axis=0).astype(BF16)
            acc_ref[h] = alpha * acc_ref[h] + _dot_nn(vt_ref[h, i], p2)
            out.append((m_new, alpha * l + jnp.sum(pa, axis=0, keepdims=True) + jnp.sum(pb, axis=0, keepdims=True)))
        return tuple(out)

    carry = lax.fori_loop(0, n, body, tuple(carry0))
    for h, sl in enumerate(heads):
        o = (acc_ref[h] * (1.0 / carry[h][1])).T
        u_ref[:, sl] = (o * _silu(g_ref[:, sl])).astype(u_ref.dtype)


def _moba_attention(z, width):
    b, s, _ = z.shape
    blk = MOBA_BLOCK
    nb = s // blk
    assert nb % 2 == 0, "query and key blocks are processed in pairs"
    tq = 2 * blk
    nbp = -(-nb // SUBLANES) * SUBLANES
    hb = min(4, width // HEAD_DIM)
    cw = hb * HEAD_DIM
    per = width // cw
    topk = min(MOBA_TOPK, nb - 1)
    kern = functools.partial(_moba_attn_kernel, hb=hb, nb=nb, blk=blk, topk=topk)
    resident = functools.partial(pl.BlockSpec, (None, s, cw), pipeline_mode=pl.Buffered(1))
    return pl.pallas_call(
        kern,
        out_shape=jax.ShapeDtypeStruct((b, s, width), BF16),
        grid=(b, per, nb // 2),
        in_specs=[pl.BlockSpec((None, tq, cw), lambda bi, g, n: (bi, n, g)),
                  resident(lambda bi, g, n: (bi, 0, per + g)),
                  resident(lambda bi, g, n: (bi, 0, 2 * per + g)),
                  pl.BlockSpec((None, tq, cw), lambda bi, g, n: (bi, n, 3 * per + g))],
        out_specs=pl.BlockSpec((None, tq, cw), lambda bi, g, n: (bi, n, g)),
        scratch_shapes=[pltpu.VMEM((s, cw), BF16), pltpu.VMEM((hb, nb // 2, HEAD_DIM, tq), BF16),
                        pltpu.VMEM((hb, nbp, HEAD_DIM), F32), pltpu.VMEM((hb, nbp, tq), F32),
                        pltpu.VMEM((hb, HEAD_DIM, tq), F32)],
        compiler_params=_params(3),
        name="moba_attn",
    )(z, z, z, z)


def _dil_sample_kernel(z_ref, k0_ref, v0_ref, k1_ref, v1_ref, k2_ref, v2_ref, u_ref):
    kv = ((k0_ref, v0_ref), (k1_ref, v1_ref), (k2_ref, v2_ref))
    outs, lses = [], []
    for g, (k_ref, v_ref) in enumerate(kv):
        q, k_new, v_new = z_ref[3 * g], z_ref[3 * g + 1], z_ref[3 * g + 2]
        s = jnp.sum(k_ref[...] * q[None], axis=-1, keepdims=True) * SCALE
        s_new = jnp.sum(k_new * q, axis=-1, keepdims=True) * SCALE
        m = jnp.maximum(jnp.max(s, axis=0), s_new)
        p = jnp.exp(s - m[None])
        p_new = jnp.exp(s_new - m)
        l = jnp.sum(p, axis=0) + p_new
        outs.append((jnp.sum(p * v_ref[...], axis=0) + p_new * v_new) * (1.0 / l))
        lses.append(m + jnp.log(l))
    u_ref[...] = _merge3(outs, lses) * _silu(z_ref[9])


def _dil_sample(zs, caches, layer):
    db, _, n_heads, _ = zs.shape
    ins, specs = [zs], [pl.BlockSpec((None,) + zs.shape[1:], lambda b: (b, 0, 0, 0))]
    for (win, dil), c in zip(DIL_GROUPS, caches):
        span = win // dil
        assert c.shape[2] == win, "cache must hold a full window"
        cr = c.reshape(c.shape[0], db, span, dil, 2, n_heads, HEAD_DIM)
        blk = (None, None, span, None, None, n_heads, HEAD_DIM)
        ins += [cr, cr]
        specs += [pl.BlockSpec(blk, lambda b: (layer, b, 0, 0, 0, 0, 0)),
                  pl.BlockSpec(blk, lambda b: (layer, b, 0, 0, 1, 0, 0))]
    return pl.pallas_call(
        _dil_sample_kernel,
        out_shape=jax.ShapeDtypeStruct((db, n_heads, HEAD_DIM), F32),
        grid=(db,),
        in_specs=specs,
        out_specs=pl.BlockSpec((None, n_heads, HEAD_DIM), lambda b: (b, 0, 0)),
        compiler_params=_params(1),
        name="dil_sample",
    )(*ins)


def _kmean_kernel(pt_ref, *refs, ppb):
    k_refs, o_ref = refs[:-1], refs[-1]
    for i in range(len(k_refs) // ppb):
        acc = jnp.sum(k_refs[i * ppb][...], axis=0)
        for p in range(1, ppb):
            acc = acc + jnp.sum(k_refs[i * ppb + p][...], axis=0)
        o_ref[i] = acc * (1.0 / (ppb * PAGE_SIZE))


def _moba_kmean(pool, layer, page_table):
    db, n_pages = page_table.shape
    n_heads = pool.shape[4]
    ppb = MOBA_BLOCK // PAGE_SIZE
    pps = 8
    assert n_pages % pps == 0 and pps % ppb == 0
    blk = (None, None, PAGE_SIZE, None, n_heads, HEAD_DIM)

    def page(i):
        return lambda b, t, pt: (layer, pt[b, t * pps + i], 0, 0, 0, 0)

    return pl.pallas_call(
        functools.partial(_kmean_kernel, ppb=ppb),
        out_shape=jax.ShapeDtypeStruct((db, n_pages // ppb, n_heads, HEAD_DIM), F32),
        grid_spec=pltpu.PrefetchScalarGridSpec(
            num_scalar_prefetch=1,
            grid=(db, n_pages // pps),
            in_specs=[pl.BlockSpec(blk, page(i)) for i in range(pps)],
            out_specs=pl.BlockSpec((None, pps // ppb, n_heads, HEAD_DIM), lambda b, t, pt: (b, t, 0, 0)),
        ),
        compiler_params=_params(2),
        name="moba_kmean",
    )(page_table, *([pool] * pps))


def _moba_topk_kernel(z_ref, km_ref, o_ref, *, topk):
    n_full = km_ref.shape[0]
    gate = jnp.sum(km_ref[...] * z_ref[0][None], axis=-1, keepdims=True)
    idx = lax.broadcasted_iota(jnp.int32, gate.shape, 0).astype(F32)
    for t in range(topk):
        best = jnp.max(gate, axis=0, keepdims=True)
        pick = jnp.min(jnp.where(gate == best, idx, float(n_full)), axis=0, keepdims=True)
        o_ref[t] = jnp.broadcast_to(pick[0], o_ref.shape[1:]).astype(jnp.int32)
        gate = jnp.where(idx == pick, -jnp.inf, gate)


def _moba_topk(zs, kmean, topk):
    db, n_full, n_heads, _ = kmean.shape
    return pl.pallas_call(
        functools.partial(_moba_topk_kernel, topk=topk),
        out_shape=jax.ShapeDtypeStruct((db, topk, n_heads, LANES), jnp.int32),
        grid=(db,),
        in_specs=[pl.BlockSpec((None,) + zs.shape[1:], lambda b: (b, 0, 0, 0)),
                  pl.BlockSpec((None, n_full, n_heads, HEAD_DIM), lambda b: (b, 0, 0, 0))],
        out_specs=pl.BlockSpec((None, topk, n_heads, LANES), lambda b: (b, 0, 0, 0)),
        compiler_params=_params(1),
        name="moba_topk",
    )(zs, kmean)


def _moba_sample_kernel(pt_ref, top_ref, z_ref, pool_ref, u_ref, kbuf, vbuf, sem,
                        *, layer, n_heads, topk, ppb, n_batch):
    b = pl.program_id(0)
    slot = b % 2

    def copies(bb, sl):
        out = []
        for h in range(n_heads):
            for t in range(topk):
                first = top_ref[bb, h * topk + t] * ppb
                for p in range(ppb):
                    page = pt_ref[bb, first + p]
                    for kvi, buf in enumerate((kbuf, vbuf)):
                        out.append(pltpu.make_async_copy(pool_ref.at[layer, page, :, kvi, h, :],
                                                         buf.at[sl, h, t * ppb + p], sem.at[sl]))
        return out

    @pl.when(b == 0)
    def _():
        for c in copies(0, 0):
            c.start()

    @pl.when(b + 1 < n_batch)
    def _():
        for c in copies(b + 1, 1 - slot):
            c.start()

    for c in copies(b, slot):
        c.wait()

    for h in range(n_heads):
        q, k_new, v_new = (z_ref[c, h:h + 1, :] for c in range(3))
        k = kbuf[slot, h]
        s = jnp.sum(k * q[None], axis=-1, keepdims=True) * SCALE
        s_own = jnp.sum(k_new * q, axis=-1, keepdims=True) * SCALE
        m = jnp.maximum(jnp.max(jnp.max(s, axis=0), axis=0, keepdims=True), s_own)
        p = jnp.exp(s - m[None])
        p_own = jnp.exp(s_own - m)
        l = jnp.sum(jnp.sum(p, axis=0), axis=0, keepdims=True) + p_own
        o = jnp.sum(jnp.sum(p * vbuf[slot, h], axis=0), axis=0, keepdims=True) + p_own * v_new
        u_ref[h:h + 1, :] = o * (1.0 / l) * _silu(z_ref[3, h:h + 1, :])


def _moba_sample(zs, pool, layer, page_table, top, topk):
    db, _, n_heads, _ = zs.shape
    ppb = MOBA_BLOCK // PAGE_SIZE
    kern = functools.partial(_moba_sample_kernel, layer=layer, n_heads=n_heads, topk=topk, ppb=ppb, n_batch=db)
    buf = pltpu.VMEM((2, n_heads, topk * ppb, PAGE_SIZE, HEAD_DIM), F32)
    return pl.pallas_call(
        kern,
        out_shape=jax.ShapeDtypeStruct((db, n_heads, HEAD_DIM), F32),
        grid_spec=pltpu.PrefetchScalarGridSpec(
            num_scalar_prefetch=2,
            grid=(db,),
            in_specs=[pl.BlockSpec((None,) + zs.shape[1:], lambda b, pt, tp: (b, 0, 0, 0)),
                      pl.BlockSpec(memory_space=pl.ANY)],
            out_specs=pl.BlockSpec((None, n_heads, HEAD_DIM), lambda b, pt, tp: (b, 0, 0)),
            scratch_shapes=[buf, buf, pltpu.SemaphoreType.DMA((2,))],
        ),
        compiler_params=_params(1),
        name="moba_sample",
    )(page_table, top, zs, pool)


def kernel(x_prompt, x_sample, cache_dil0, cache_dil1, cache_dil2, cache_moba, page_table,
           norm_dil, w_in_dil, w_out_dil, norm_moba, w_in_moba, w_out_moba, final_norm):
    b, s, d = x_prompt.shape
    db, t, _ = x_sample.shape
    width = w_out_dil.shape[1]
    n_heads = width // HEAD_DIM
    depth = norm_dil.shape[0] + norm_moba.shape[0]
    n_pages = page_table.shape[1]
    assert t == 1 and n_pages % (MOBA_BLOCK // PAGE_SIZE) == 0, "decode step: one token, no partial key block"
    dil_caches = (cache_dil0, cache_dil1, cache_dil2)

    def residue_major(a, dil):
        return jnp.swapaxes(a.reshape((a.shape[0], a.shape[1] // dil, dil) + a.shape[2:]), 1, 2)

    def token_major(a):
        a = jnp.swapaxes(a, 1, 2)
        return a.reshape((a.shape[0] * a.shape[1] * a.shape[2],) + a.shape[3:])

    pos = jnp.arange(s)
    rope_p = [_rope_or_identity(residue_major(pos[None], dil).reshape(s)) for _, dil in DIL_GROUPS]
    rope_s = _rope_or_identity(jnp.full((db,), n_pages * PAGE_SIZE))
    dil_rope = tuple(c for c in range(9) if c % 3 != 2)
    project = functools.partial(_matmul, width=width, name="proj_rope")

    def out_project(u, w, layer, x):
        return _matmul(u, w, layer, 0, w.shape[2], width=width, residual=x, name="out_proj")

    xp = x_prompt.reshape(b * s, d)
    xs = x_sample.reshape(db, d)
    dil_new_p = [[] for _ in DIL_GROUPS]
    dil_new_s = [[] for _ in DIL_GROUPS]
    moba_new_p, moba_new_s = [], []
    kmeans = {}
    for i in range(depth):
        j = i // 2
        if i % 2 == 0:
            w_in, w_out = w_in_dil, w_out_dil
            n_cols = w_in.shape[2] // width
            hp = _rmsnorm(xp, norm_dil[j], BF16)
            hs = _rmsnorm(xs, norm_dil[j], F32)
            zs = project(hs, w_in, j, 0, n_cols * width, tables=rope_s, rope_cols=dil_rope)
            outs, km_parts, km_done = [], [], 0
            km_total = db * n_pages // PAGES_PER_STEP
            for g, (win, dil) in enumerate(DIL_GROUPS):
                hg = residue_major(hp.reshape(b, s, d), dil).reshape(b * s, d)
                pages = (cache_moba, j, page_table, km_done) if j < cache_moba.shape[0] and km_done < km_total else None
                n_grp, tail = (4, n_cols - 1) if dil == 1 else (3, None)
                zg = project(hg, w_in, j, 3 * g, n_grp * width, tail_col=tail, tables=rope_p[g], rope_cols=(0, 1),
                             pages=pages)
                if pages is not None:
                    zg, km = zg
                    km_parts.append(km)
                    km_done += km.shape[0] * (MOBA_BLOCK // PAGE_SIZE) // PAGES_PER_STEP
                if dil == 1:
                    gate = zg
                zg = zg.reshape(b, dil, s // dil, n_grp * width)
                outs.append(token_major(_dil_attention(zg, win // dil, width)))
                keep = min(win, s)
                kv = zg[:, :, (s - keep) // dil:, width:3 * width]
                dil_new_p[g].append(token_major(kv).reshape(b, keep, 2, n_heads, HEAD_DIM))
                kv_s = zs[:, (3 * g + 1) * width:(3 * g + 3) * width]
                dil_new_s[g].append(kv_s.reshape(db, 1, 2, n_heads, HEAD_DIM))
            up = _merge_gate(outs, gate, 3, width)
            us = _dil_sample(zs.reshape(db, n_cols, n_heads, HEAD_DIM), dil_caches, j)
            xp = out_project(up, w_out, j, xp)
            xs = out_project(us.reshape(db, width), w_out, j, xs)
            if km_parts and km_done == km_total:
                kmeans[j] = jnp.concatenate(km_parts).reshape(db, -1, n_heads, HEAD_DIM)
        else:
            w_in, w_out = w_in_moba, w_out_moba
            zp = project(_rmsnorm(xp, norm_moba[j], BF16), w_in, j, 0, 4 * width, tables=rope_p[0], rope_cols=(0, 1))
            zs = project(_rmsnorm(xs, norm_moba[j], F32), w_in, j, 0, 4 * width, tables=rope_s, rope_cols=(0, 1))
            zp3 = zp.reshape(b, s, 4 * width)
            zs4 = zs.reshape(db, 4, n_heads, HEAD_DIM)
            up = _moba_attention(zp3, width)
            kmean = kmeans[j] if j in kmeans else _moba_kmean(cache_moba, j, page_table)
            topk = min(MOBA_TOPK, kmean.shape[1])
            top = _moba_topk(zs4, kmean, topk)[:, :, :, 0]
            top = jnp.swapaxes(top, 1, 2).reshape(db, n_heads * topk)
            us = _moba_sample(zs4, cache_moba, j, page_table, top, topk)
            xp = out_project(up.reshape(b * s, width), w_out, j, xp)
            xs = out_project(us.reshape(db, width), w_out, j, xs)
            moba_new_p.append(zp3[:, :, width:3 * width].reshape(b, s, 2, n_heads, HEAD_DIM))
            moba_new_s.append(zs[:, width:3 * width].reshape(db, 1, 2, n_heads, HEAD_DIM))
    y_prompt = _rmsnorm(xp, final_norm, F32).reshape(b, s, d)
    y_sample = _rmsnorm(xs, final_norm, F32).reshape(db, 1, d)
    return (y_prompt, y_sample,
            jnp.stack(dil_new_p[0]), jnp.stack(dil_new_s[0]),
            jnp.stack(dil_new_p[1]), jnp.stack(dil_new_s[1]),
            jnp.stack(dil_new_p[2]), jnp.stack(dil_new_s[2]),
            jnp.stack(moba_new_p), jnp.stack(moba_new_s))
```

```python
import functools
import math

import jax
import jax.numpy as jnp
from jax import lax
from jax.experimental import pallas as pl
from jax.experimental.pallas import tpu as pltpu

HEAD_DIM = 128
DIL_GROUPS = ((128, 1), (512, 4), (2048, 16))
MOBA_BLOCK = 256
MOBA_TOPK = 3
PAGE_SIZE = 128
ROPE_THETA = 10000.0
RMS_EPS = 1e-6
SCALE = HEAD_DIM ** -0.5
EXP2_SCALE = SCALE * math.log2(math.e)
NEG = -1e30

LANES = 128
SUBLANES = 8
MXU_COLS = 256
PAGES_PER_STEP = 8
VMEM_LIMIT = 48 * 1024 * 1024
MATMUL_VMEM_LIMIT = 56 * 1024 * 1024

F32 = jnp.float32
BF16 = jnp.bfloat16


def _params(n_axes, vmem=VMEM_LIMIT):
    return pltpu.CompilerParams(dimension_semantics=("arbitrary",) * n_axes, vmem_limit_bytes=vmem)


def _silu(g):
    return g * (1.0 / (1.0 + jnp.exp(-g)))


def _dot_nn(a, b):
    return jnp.dot(a, b, preferred_element_type=F32)


def _dot_nt(a, b, precision=None):
    return lax.dot_general(a, b, (((1,), (1,)), ((), ())), preferred_element_type=F32, precision=precision)


def _merge3(outs, lses):
    mx = jnp.maximum(jnp.maximum(lses[0], lses[1]), lses[2])
    es = [jnp.exp(x - mx) for x in lses]
    inv = 1.0 / (es[0] + es[1] + es[2])
    return (es[0] * inv) * outs[0] + (es[1] * inv) * outs[1] + (es[2] * inv) * outs[2]


def _rmsnorm_kernel(x_ref, g_ref, o_ref):
    x = x_ref[...]
    ms = jnp.mean(x * x, axis=-1, keepdims=True)
    o_ref[...] = (x * lax.rsqrt(ms + RMS_EPS) * g_ref[...]).astype(o_ref.dtype)


def _rmsnorm(x, g, out_dtype):
    m, d = x.shape
    tm = min(m, 512)
    return pl.pallas_call(
        _rmsnorm_kernel,
        out_shape=jax.ShapeDtypeStruct((m, d), out_dtype),
        grid=(m // tm,),
        in_specs=[pl.BlockSpec((tm, d), lambda i: (i, 0)), pl.BlockSpec((1, d), lambda i: (0, 0))],
        out_specs=pl.BlockSpec((tm, d), lambda i: (i, 0)),
        compiler_params=_params(1),
        name="rmsnorm",
    )(x, g.reshape(1, d))


def _rope_tables(pos):
    half = HEAD_DIM // 2
    inv = ROPE_THETA ** (-jnp.arange(half, dtype=F32) / half)
    ang = pos.astype(F32)[:, None] * inv[None, :]
    cos, sin = jnp.cos(ang), jnp.sin(ang)
    return jnp.concatenate([cos, cos], axis=-1), jnp.concatenate([-sin, sin], axis=-1)


def _rope_or_identity(pos):
    cos, sin = _rope_tables(pos)
    rot = jnp.concatenate([cos, sin], axis=-1)
    keep = jnp.concatenate([jnp.ones_like(cos), jnp.zeros_like(sin)], axis=-1)
    return jnp.stack([rot, keep])


def _matmul_kernel(*refs, rope, residual, km_steps, grid_steps):
    if km_steps:
        refs = refs[1:]
    a_ref, w_ref, *rest = refs
    wb_ref = rest.pop()
    if km_steps:
        km_ref = rest.pop()
    o_ref = rest.pop()
    if rope:
        table_ref, *rest = rest
    elif residual:
        x_ref, *rest = rest
    page_refs = rest

    @pl.when(pl.program_id(1) == 0)
    def _():
        wb_ref[...] = w_ref[...].astype(BF16)

    if km_steps:
        ppb = MOBA_BLOCK // PAGE_SIZE

        def block_means():
            for blk in range(len(page_refs) // ppb):
                acc = jnp.sum(page_refs[blk * ppb][...], axis=0)
                for p in range(1, ppb):
                    acc = acc + jnp.sum(page_refs[blk * ppb + p][...], axis=0)
                km_ref[blk] = acc * (1.0 / MOBA_BLOCK)

        if km_steps == grid_steps:
            block_means()
        else:
            pl.when(pl.program_id(0) * pl.num_programs(1) + pl.program_id(1) < km_steps)(block_means)

    a = a_ref[...].astype(BF16)
    for c in range(o_ref.shape[1] // MXU_COLS):
        cols = slice(c * MXU_COLS, (c + 1) * MXU_COLS)
        acc = _dot_nn(a, wb_ref[:, cols])
        if rope:
            cos, sin = table_ref[:, :LANES], table_ref[:, LANES:]
            for lo in range(0, MXU_COLS, HEAD_DIM):
                x = acc[:, lo:lo + HEAD_DIM]
                out = x * cos + pltpu.roll(x, HEAD_DIM // 2, 1) * sin
                o_ref[:, c * MXU_COLS + lo:c * MXU_COLS + lo + HEAD_DIM] = out
        elif residual:
            o_ref[:, cols] = x_ref[:, cols] + acc
        else:
            o_ref[:, cols] = acc


def _matmul(a, w, layer, col0, n_out, *, width, tail_col=None, tables=None, rope_cols=(), residual=None,
            pages=None, name):
    m, k = a.shape
    tm = min(m, 1024)
    tn = 1024
    off = col0 * width // tn
    n_i = m // tm
    grid = (n_out // tn, n_i)
    tail_at = (n_out - width) // tn

    def weight_tile(j, i, *_):
        skip = 0 if tail_col is None else (tail_col - (col0 + n_out // width - 1)) * (width // tn)
        return (layer, 0, off + j + (j >= tail_at).astype(jnp.int32) * skip)

    ins = [a, w]
    specs = [pl.BlockSpec((tm, k), lambda j, i, *_: (i, 0)),
             pl.BlockSpec((None, k, tn), weight_tile, pipeline_mode=pl.Buffered(1))]
    if tables is not None:
        assert tables.shape[1] % tm == 0, "row tiles must not straddle sequences"
        pos_blocks = tables.shape[1] // tm

        def table_block(j, i, *_):
            col = (j * tn) // width
            is_rope = sum((col == c).astype(jnp.int32) for c in rope_cols)
            return (1 - is_rope, i % pos_blocks, 0)

        ins.append(tables)
        specs.append(pl.BlockSpec((None, tm, 2 * LANES), table_block))
    if residual is not None:
        ins.append(residual)
        specs.append(pl.BlockSpec((tm, tn), lambda j, i, *_: (i, j)))
    out_shape = jax.ShapeDtypeStruct((m, n_out), F32)
    out_specs = pl.BlockSpec((tm, tn), lambda j, i, *_: (i, j))
    km_steps = 0
    prefetch = []
    if pages is not None:
        pool, pool_layer, page_table, first_step = pages
        db, n_pages = page_table.shape
        n_heads = pool.shape[4]
        steps_per_seq = n_pages // PAGES_PER_STEP
        km_steps = min(grid[0] * grid[1], db * steps_per_seq - first_step)
        blocks_per_step = PAGES_PER_STEP // (MOBA_BLOCK // PAGE_SIZE)

        def page(slot):
            def index(j, i, pt):
                step = first_step + jnp.minimum(j * n_i + i, km_steps - 1)
                return (pool_layer, pt[step // steps_per_seq, (step % steps_per_seq) * PAGES_PER_STEP + slot], 0, 0, 0, 0)
            return index

        prefetch = [page_table]
        ins += [pool] * PAGES_PER_STEP
        specs += [pl.BlockSpec((None, None, PAGE_SIZE, None, n_heads, HEAD_DIM), page(s)) for s in range(PAGES_PER_STEP)]
        out_shape = (out_shape, jax.ShapeDtypeStruct((km_steps * blocks_per_step, n_heads, HEAD_DIM), F32))
        out_specs = (out_specs, pl.BlockSpec((blocks_per_step, n_heads, HEAD_DIM),
                                             lambda j, i, pt: (jnp.minimum(j * n_i + i, km_steps - 1), 0, 0)))
    kern = functools.partial(_matmul_kernel, rope=tables is not None, residual=residual is not None,
                             km_steps=km_steps, grid_steps=grid[0] * grid[1])
    return pl.pallas_call(
        kern,
        out_shape=out_shape,
        grid_spec=pltpu.PrefetchScalarGridSpec(
            num_scalar_prefetch=len(prefetch), grid=grid, in_specs=specs, out_specs=out_specs,
            scratch_shapes=[pltpu.VMEM((k, tn), BF16)]),
        compiler_params=_params(2, MATMUL_VMEM_LIMIT),
        name=name,
    )(*prefetch, *ins)


def _dil_attn_kernel(z_ref, o_ref, kprev_ref, vtprev_ref, lse_s, *, n_heads, span):
    n = pl.program_id(2)
    width = n_heads * HEAD_DIM

    @pl.when(n == 0)
    def _():
        kprev_ref[...] = jnp.zeros_like(kprev_ref)
        vtprev_ref[...] = jnp.zeros_like(vtprev_ref)
        lse_s[...] = jnp.zeros_like(lse_s)

    key = lax.broadcasted_iota(jnp.int32, (2 * span, span), 0)
    qry = lax.broadcasted_iota(jnp.int32, (2 * span, span), 1)
    prev_ok = jnp.logical_and(jnp.logical_and(key < span, key >= qry), n > 0)
    mask = jnp.logical_or(prev_ok, jnp.logical_and(key >= span, key - span <= qry))
    for h in range(n_heads):
        sl = slice(h * HEAD_DIM, (h + 1) * HEAD_DIM)
        q = z_ref[:, sl].astype(BF16)
        kc = z_ref[:, width + h * HEAD_DIM:width + (h + 1) * HEAD_DIM].astype(BF16)
        vtc = z_ref[:, 2 * width + h * HEAD_DIM:2 * width + (h + 1) * HEAD_DIM].T.astype(BF16)
        s = _dot_nt(jnp.concatenate([kprev_ref[:, sl], kc], axis=0), q)
        s = jnp.where(mask, s, NEG)
        m = jnp.max(s, axis=0, keepdims=True)
        p = jnp.exp2((s - m) * EXP2_SCALE)
        l = jnp.sum(p, axis=0, keepdims=True)
        pn = (p * (1.0 / l)).astype(BF16)
        ot = _dot_nn(jnp.concatenate([vtprev_ref[h], vtc], axis=1), pn)
        o_ref[:, sl] = ot.T
        lse_s[h:h + 1, :] = m * SCALE + jnp.log(l)
        kprev_ref[:, sl] = kc
        vtprev_ref[h] = vtc
    o_ref[:, width:] = lse_s[...].T


def _dil_attention(z, span, width):
    b, dil, seq, _ = z.shape
    n_heads = width // HEAD_DIM
    assert span == LANES and seq % span == 0
    kern = functools.partial(_dil_attn_kernel, n_heads=n_heads, span=span)
    return pl.pallas_call(
        kern,
        out_shape=jax.ShapeDtypeStruct((b, dil, seq, width + LANES), F32),
        grid=(b, dil, seq // span),
        in_specs=[pl.BlockSpec((None, None, span, 3 * width), lambda bi, r, n: (bi, r, n, 0))],
        out_specs=pl.BlockSpec((None, None, span, width + LANES), lambda bi, r, n: (bi, r, n, 0)),
        scratch_shapes=[pltpu.VMEM((span, width), BF16), pltpu.VMEM((n_heads, HEAD_DIM, span), BF16),
                        pltpu.VMEM((LANES, span), F32)],
        compiler_params=_params(3),
        name=f"dil_attn_d{dil}",
    )(z)


def _merge_gate_kernel(o0_ref, o1_ref, o2_ref, g_ref, u_ref, *, n_heads):
    o_refs = (o0_ref, o1_ref, o2_ref)
    width = n_heads * HEAD_DIM
    lses = [r[:, width:] for r in o_refs]
    for h in range(n_heads):
        sl = slice(h * HEAD_DIM, (h + 1) * HEAD_DIM)
        o = _merge3([r[:, sl] for r in o_refs], [x[:, h:h + 1] for x in lses])
        u_ref[:, sl] = (o * _silu(g_ref[:, sl])).astype(u_ref.dtype)


def _merge_gate(outs, gate, gate_col, width):
    m = gate.shape[0]
    tm = min(m, 256)
    n_heads = width // HEAD_DIM
    row = lambda i: (i, 0)
    kern = functools.partial(_merge_gate_kernel, n_heads=n_heads)
    return pl.pallas_call(
        kern,
        out_shape=jax.ShapeDtypeStruct((m, width), BF16),
        grid=(m // tm,),
        in_specs=[pl.BlockSpec((tm, width + LANES), row)] * 3 + [pl.BlockSpec((tm, width), lambda i: (i, gate_col))],
        out_specs=pl.BlockSpec((tm, width), row),
        compiler_params=_params(1),
        name="dil_merge_gate",
    )(*outs, gate)


def _moba_attn_kernel(q_ref, k_ref, v_ref, g_ref, u_ref, kb_ref, vt_ref, kmean_ref, sel_ref, acc_ref,
                      *, hb, nb, blk, topk):
    n = pl.program_id(2)
    heads = [slice(h * HEAD_DIM, (h + 1) * HEAD_DIM) for h in range(hb)]

    @pl.when(n == 0)
    def _():
        kb_ref[...] = k_ref[...].astype(BF16)
        kmean_ref[...] = jnp.zeros_like(kmean_ref)
        for h, sl in enumerate(heads):
            for j in range(nb):
                rows = slice(j * blk, (j + 1) * blk)
                vt_ref[h, j // 2, :, (j % 2) * blk:(j % 2 + 1) * blk] = v_ref[rows, sl].T.astype(BF16)
                kmean_ref[h, j:j + 1, :] = jnp.sum(k_ref[rows, sl], axis=0, keepdims=True) * (1.0 / blk)

    tq = 2 * blk
    nbp = kmean_ref.shape[1]
    blkid = lax.broadcasted_iota(jnp.int32, (nbp, tq), 0)
    upper = (lax.broadcasted_iota(jnp.int32, (nbp, tq), 1) >= blk).astype(jnp.int32)
    past = blkid < 2 * n + upper
    key = lax.broadcasted_iota(jnp.int32, (tq, tq), 0)
    qry = lax.broadcasted_iota(jnp.int32, (tq, tq), 1)
    causal = jnp.logical_and((key >= blk) == (qry >= blk), key <= qry)
    lower_key = key < blk
    pair = pl.ds(pl.multiple_of(n * tq, tq), tq)
    qbs, carry0 = [], []
    for h, sl in enumerate(heads):
        qf = q_ref[:, sl]
        qb = qf.astype(BF16)
        qbs.append(qb)
        gate = _dot_nt(kmean_ref[h], qf, precision=lax.Precision.HIGHEST)
        gate = jnp.where(past, gate, NEG)
        rank = jnp.zeros((nbp, tq), jnp.int32)
        for j in range(nb):
            gj = gate[j:j + 1, :]
            beats = jnp.logical_or(gj > gate, jnp.logical_and(gj == gate, blkid > j))
            rank = rank + beats.astype(jnp.int32)
        sel_ref[h] = jnp.where(jnp.logical_and(rank < topk, past), 1.0, 0.0)

        picked = sel_ref[h, pl.ds(2 * n, 1), :] > 0.0
        s = _dot_nt(kb_ref[pair, sl], qb)
        s = jnp.where(jnp.logical_or(causal, jnp.logical_and(lower_key, picked)), s, NEG)
        m0 = jnp.max(s, axis=0, keepdims=True)
        p = jnp.exp2((s - m0) * EXP2_SCALE)
        acc_ref[h] = _dot_nn(vt_ref[h, n], p.astype(BF16))
        carry0.append((m0, jnp.sum(p, axis=0, keepdims=True)))

    def body(i, carry):
        rows = pl.ds(pl.multiple_of(i * tq, tq), tq)
        out = []
        for h, sl in enumerate(heads):
            m, l = carry[h]
            s = _dot_nt(kb_ref[rows, sl], qbs[h])
            sa = jnp.where(sel_ref[h, pl.ds(2 * i, 1), :] > 0.0, s[:blk], NEG)
            sb = jnp.where(sel_ref[h, pl.ds(2 * i + 1, 1), :] > 0.0, s[blk:], NEG)
            m_new = jnp.maximum(m, jnp.maximum(jnp.max(sa, axis=0, keepdims=True), jnp.max(sb, axis=0, keepdims=True)))
            alpha = jnp.exp2((m - m_new) * EXP2_SCALE)
            pa = jnp.exp2((sa - m_new) * EXP2_SCALE)
            pb = jnp.exp2((sb - m_new) * EXP2_SCALE)
            p2 = jnp.concatenate([pa, pb], axis=0).astype(BF16)
            acc_ref[h] = alpha * acc_ref[h] + _dot_nn(vt_ref[h, i], p2)
            out.append((m_new, alpha * l + jnp.sum(pa, axis=0, keepdims=True) + jnp.sum(pb, axis=0, keepdims=True)))
        return tuple(out)

    carry = lax.fori_loop(0, n, body, tuple(carry0))
    for h, sl in enumerate(heads):
        o = (acc_ref[h] * (1.0 / carry[h][1])).T
        u_ref[:, sl] = (o * _silu(g_ref[:, sl])).astype(u_ref.dtype)


def _moba_attention(zqg, zkv, width):
    b, s, _ = zqg.shape
    blk = MOBA_BLOCK
    nb = s // blk
    assert nb % 2 == 0, "query and key blocks are processed in pairs"
    tq = 2 * blk
    nbp = -(-nb // SUBLANES) * SUBLANES
    hb = min(4, width // HEAD_DIM)
    cw = hb * HEAD_DIM
    per = width // cw
    topk = min(MOBA_TOPK, nb - 1)
    kern = functools.partial(_moba_attn_kernel, hb=hb, nb=nb, blk=blk, topk=topk)
    resident = functools.partial(pl.BlockSpec, (None, s, cw), pipeline_mode=pl.Buffered(1))
    return pl.pallas_call(
        kern,
        out_shape=jax.ShapeDtypeStruct((b, s, width), BF16),
        grid=(b, per, nb // 2),
        in_specs=[pl.BlockSpec((None, tq, cw), lambda bi, g, n: (bi, n, g)),
                  resident(lambda bi, g, n: (bi, 0, g)),
                  resident(lambda bi, g, n: (bi, 0, per + g)),
                  pl.BlockSpec((None, tq, cw), lambda bi, g, n: (bi, n, per + g))],
        out_specs=pl.BlockSpec((None, tq, cw), lambda bi, g, n: (bi, n, g)),
        scratch_shapes=[pltpu.VMEM((s, cw), BF16), pltpu.VMEM((hb, nb // 2, HEAD_DIM, tq), BF16),
                        pltpu.VMEM((hb, nbp, HEAD_DIM), F32), pltpu.VMEM((hb, nbp, tq), F32),
                        pltpu.VMEM((hb, HEAD_DIM, tq), F32)],
        compiler_params=_params(3),
        name="moba_attn",
    )(zqg, zkv, zkv, zqg)


def _dil_sample_kernel(z_ref, k0_ref, v0_ref, k1_ref, v1_ref, k2_ref, v2_ref, u_ref):
    kv = ((k0_ref, v0_ref), (k1_ref, v1_ref), (k2_ref, v2_ref))
    outs, lses = [], []
    for g, (k_ref, v_ref) in enumerate(kv):
        q, k_new, v_new = z_ref[3 * g], z_ref[3 * g + 1], z_ref[3 * g + 2]
        s = jnp.sum(k_ref[...] * q[None], axis=-1, keepdims=True) * SCALE
        s_new = jnp.sum(k_new * q, axis=-1, keepdims=True) * SCALE
        m = jnp.maximum(jnp.max(s, axis=0), s_new)
        p = jnp.exp(s - m[None])
        p_new = jnp.exp(s_new - m)
        l = jnp.sum(p, axis=0) + p_new
        outs.append((jnp.sum(p * v_ref[...], axis=0) + p_new * v_new) * (1.0 / l))
        lses.append(m + jnp.log(l))
    u_ref[...] = _merge3(outs, lses) * _silu(z_ref[9])


def _dil_sample(zs, caches, layer):
    db, _, n_heads, _ = zs.shape
    ins, specs = [zs], [pl.BlockSpec((None,) + zs.shape[1:], lambda b: (b, 0, 0, 0))]
    for (win, dil), c in zip(DIL_GROUPS, caches):
        span = win // dil
        assert c.shape[2] == win, "cache must hold a full window"
        cr = c.reshape(c.shape[0], db, span, dil, 2, n_heads, HEAD_DIM)
        blk = (None, None, span, None, None, n_heads, HEAD_DIM)
        ins += [cr, cr]
        specs += [pl.BlockSpec(blk, lambda b: (layer, b, 0, 0, 0, 0, 0)),
                  pl.BlockSpec(blk, lambda b: (layer, b, 0, 0, 1, 0, 0))]
    return pl.pallas_call(
        _dil_sample_kernel,
        out_shape=jax.ShapeDtypeStruct((db, n_heads, HEAD_DIM), F32),
        grid=(db,),
        in_specs=specs,
        out_specs=pl.BlockSpec((None, n_heads, HEAD_DIM), lambda b: (b, 0, 0)),
        compiler_params=_params(1),
        name="dil_sample",
    )(*ins)


def _kmean_kernel(pt_ref, *refs, ppb):
    k_refs, o_ref = refs[:-1], refs[-1]
    for i in range(len(k_refs) // ppb):
        acc = jnp.sum(k_refs[i * ppb][...], axis=0)
        for p in range(1, ppb):
            acc = acc + jnp.sum(k_refs[i * ppb + p][...], axis=0)
        o_ref[i] = acc * (1.0 / (ppb * PAGE_SIZE))


def _moba_kmean(pool, layer, page_table):
    db, n_pages = page_table.shape
    n_heads = pool.shape[4]
    ppb = MOBA_BLOCK // PAGE_SIZE
    pps = 8
    assert n_pages % pps == 0 and pps % ppb == 0
    blk = (None, None, PAGE_SIZE, None, n_heads, HEAD_DIM)

    def page(i):
        return lambda b, t, pt: (layer, pt[b, t * pps + i], 0, 0, 0, 0)

    return pl.pallas_call(
        functools.partial(_kmean_kernel, ppb=ppb),
        out_shape=jax.ShapeDtypeStruct((db, n_pages // ppb, n_heads, HEAD_DIM), F32),
        grid_spec=pltpu.PrefetchScalarGridSpec(
            num_scalar_prefetch=1,
            grid=(db, n_pages // pps),
            in_specs=[pl.BlockSpec(blk, page(i)) for i in range(pps)],
            out_specs=pl.BlockSpec((None, pps // ppb, n_heads, HEAD_DIM), lambda b, t, pt: (b, t, 0, 0)),
        ),
        compiler_params=_params(2),
        name="moba_kmean",
    )(page_table, *([pool] * pps))


def _moba_topk_kernel(z_ref, km_ref, o_ref, *, topk):
    n_full = km_ref.shape[0]
    gate = jnp.sum(km_ref[...] * z_ref[0][None], axis=-1, keepdims=True)
    idx = lax.broadcasted_iota(jnp.int32, gate.shape, 0).astype(F32)
    for t in range(topk):
        best = jnp.max(gate, axis=0, keepdims=True)
        pick = jnp.min(jnp.where(gate == best, idx, float(n_full)), axis=0, keepdims=True)
        o_ref[t] = jnp.broadcast_to(pick[0], o_ref.shape[1:]).astype(jnp.int32)
        gate = jnp.where(idx == pick, -jnp.inf, gate)


def _moba_topk(zs, kmean, topk):
    db, n_full, n_heads, _ = kmean.shape
    return pl.pallas_call(
        functools.partial(_moba_topk_kernel, topk=topk),
        out_shape=jax.ShapeDtypeStruct((db, topk, n_heads, LANES), jnp.int32),
        grid=(db,),
        in_specs=[pl.BlockSpec((None,) + zs.shape[1:], lambda b: (b, 0, 0, 0)),
                  pl.BlockSpec((None, n_full, n_heads, HEAD_DIM), lambda b: (b, 0, 0, 0))],
        out_specs=pl.BlockSpec((None, topk, n_heads, LANES), lambda b: (b, 0, 0, 0)),
        compiler_params=_params(1),
        name="moba_topk",
    )(zs, kmean)


def _moba_sample_kernel(pt_ref, top_ref, z_ref, pool_ref, u_ref, kbuf, vbuf, sem,
                        *, layer, n_heads, topk, ppb, n_batch):
    b = pl.program_id(0)
    slot = b % 2

    def copies(bb, sl):
        out = []
        for h in range(n_heads):
            for t in range(topk):
                first = top_ref[bb, h * topk + t] * ppb
                for p in range(ppb):
                    page = pt_ref[bb, first + p]
                    for kvi, buf in enumerate((kbuf, vbuf)):
                        out.append(pltpu.make_async_copy(pool_ref.at[layer, page, :, kvi, h, :],
                                                         buf.at[sl, h, t * ppb + p], sem.at[sl]))
        return out

    @pl.when(b == 0)
    def _():
        for c in copies(0, 0):
            c.start()

    @pl.when(b + 1 < n_batch)
    def _():
        for c in copies(b + 1, 1 - slot):
            c.start()

    for c in copies(b, slot):
        c.wait()

    for h in range(n_heads):
        q, k_new, v_new = (z_ref[c, h:h + 1, :] for c in range(3))
        k = kbuf[slot, h]
        s = jnp.sum(k * q[None], axis=-1, keepdims=True) * SCALE
        s_own = jnp.sum(k_new * q, axis=-1, keepdims=True) * SCALE
        m = jnp.maximum(jnp.max(jnp.max(s, axis=0), axis=0, keepdims=True), s_own)
        p = jnp.exp(s - m[None])
        p_own = jnp.exp(s_own - m)
        l = jnp.sum(jnp.sum(p, axis=0), axis=0, keepdims=True) + p_own
        o = jnp.sum(jnp.sum(p * vbuf[slot, h], axis=0), axis=0, keepdims=True) + p_own * v_new
        u_ref[h:h + 1, :] = o * (1.0 / l) * _silu(z_ref[3, h:h + 1, :])


def _moba_sample(zs, pool, layer, page_table, top, topk):
    db, _, n_heads, _ = zs.shape
    ppb = MOBA_BLOCK // PAGE_SIZE
    kern = functools.partial(_moba_sample_kernel, layer=layer, n_heads=n_heads, topk=topk, ppb=ppb, n_batch=db)
    buf = pltpu.VMEM((2, n_heads, topk * ppb, PAGE_SIZE, HEAD_DIM), F32)
    return pl.pallas_call(
        kern,
        out_shape=jax.ShapeDtypeStruct((db, n_heads, HEAD_DIM), F32),
        grid_spec=pltpu.PrefetchScalarGridSpec(
            num_scalar_prefetch=2,
            grid=(db,),
            in_specs=[pl.BlockSpec((None,) + zs.shape[1:], lambda b, pt, tp: (b, 0, 0, 0)),
                      pl.BlockSpec(memory_space=pl.ANY)],
            out_specs=pl.BlockSpec((None, n_heads, HEAD_DIM), lambda b, pt, tp: (b, 0, 0)),
            scratch_shapes=[buf, buf, pltpu.SemaphoreType.DMA((2,))],
        ),
        compiler_params=_params(1),
        name="moba_sample",
    )(page_table, top, zs, pool)


def kernel(x_prompt, x_sample, cache_dil0, cache_dil1, cache_dil2, cache_moba, page_table,
           norm_dil, w_in_dil, w_out_dil, norm_moba, w_in_moba, w_out_moba, final_norm):
    b, s, d = x_prompt.shape
    db, t, _ = x_sample.shape
    width = w_out_dil.shape[1]
    n_heads = width // HEAD_DIM
    depth = norm_dil.shape[0] + norm_moba.shape[0]
    n_pages = page_table.shape[1]
    assert t == 1 and n_pages % (MOBA_BLOCK // PAGE_SIZE) == 0, "decode step: one token, no partial key block"
    dil_caches = (cache_dil0, cache_dil1, cache_dil2)

    def residue_major(a, dil):
        return jnp.swapaxes(a.reshape((a.shape[0], a.shape[1] // dil, dil) + a.shape[2:]), 1, 2)

    def token_major(a):
        a = jnp.swapaxes(a, 1, 2)
        return a.reshape((a.shape[0] * a.shape[1] * a.shape[2],) + a.shape[3:])

    pos = jnp.arange(s)
    rope_p = [_rope_or_identity(residue_major(pos[None], dil).reshape(s)) for _, dil in DIL_GROUPS]
    rope_s = _rope_or_identity(jnp.full((db,), n_pages * PAGE_SIZE))
    dil_rope = tuple(c for c in range(9) if c % 3 != 2)
    project = functools.partial(_matmul, width=width, name="proj_rope")

    def out_project(u, w, layer, x):
        return _matmul(u, w, layer, 0, w.shape[2], width=width, residual=x, name="out_proj")

    xp = x_prompt.reshape(b * s, d)
    xs = x_sample.reshape(db, d)
    dil_new_p = [[] for _ in DIL_GROUPS]
    dil_new_s = [[] for _ in DIL_GROUPS]
    moba_new_p, moba_new_s = [], []
    kmeans = {}
    for i in range(depth):
        j = i // 2
        if i % 2 == 0:
            w_in, w_out = w_in_dil, w_out_dil
            n_cols = w_in.shape[2] // width
            hp = _rmsnorm(xp, norm_dil[j], BF16)
            hs = _rmsnorm(xs, norm_dil[j], F32)
            zs = project(hs, w_in, j, 0, n_cols * width, tables=rope_s, rope_cols=dil_rope)
            outs, km_parts, km_done = [], [], 0
            km_total = db * n_pages // PAGES_PER_STEP
            for g, (win, dil) in enumerate(DIL_GROUPS):
                hg = residue_major(hp.reshape(b, s, d), dil).reshape(b * s, d)
                pages = (cache_moba, j, page_table, km_done) if j < cache_moba.shape[0] and km_done < km_total else None
                n_grp, tail = (4, n_cols - 1) if dil == 1 else (3, None)
                zg = project(hg, w_in, j, 3 * g, n_grp * width, tail_col=tail, tables=rope_p[g], rope_cols=(0, 1),
                             pages=pages)
                if pages is not None:
                    zg, km = zg
                    km_parts.append(km)
                    km_done += km.shape[0] * (MOBA_BLOCK // PAGE_SIZE) // PAGES_PER_STEP
                if dil == 1:
                    gate = zg
                zg = zg.reshape(b, dil, s // dil, n_grp * width)
                outs.append(token_major(_dil_attention(zg, win // dil, width)))
                keep = min(win, s)
                kv = zg[:, :, (s - keep) // dil:, width:3 * width]
                dil_new_p[g].append(token_major(kv).reshape(b, keep, 2, n_heads, HEAD_DIM))
                kv_s = zs[:, (3 * g + 1) * width:(3 * g + 3) * width]
                dil_new_s[g].append(kv_s.reshape(db, 1, 2, n_heads, HEAD_DIM))
            up = _merge_gate(outs, gate, 3, width)
            us = _dil_sample(zs.reshape(db, n_cols, n_heads, HEAD_DIM), dil_caches, j)
            xp = out_project(up, w_out, j, xp)
            xs = out_project(us.reshape(db, width), w_out, j, xs)
            if km_parts and km_done == km_total:
                kmeans[j] = jnp.concatenate(km_parts).reshape(db, -1, n_heads, HEAD_DIM)
        else:
            w_in, w_out = w_in_moba, w_out_moba
            hp = _rmsnorm(xp, norm_moba[j], BF16)
            zqg = project(hp, w_in, j, 0, 2 * width, tail_col=3, tables=rope_p[0], rope_cols=(0,))
            zkv = project(hp, w_in, j, 1, 2 * width, tables=rope_p[0], rope_cols=(0,))
            zs = project(_rmsnorm(xs, norm_moba[j], F32), w_in, j, 0, 4 * width, tables=rope_s, rope_cols=(0, 1))
            zs4 = zs.reshape(db, 4, n_heads, HEAD_DIM)
            up = _moba_attention(zqg.reshape(b, s, 2 * width), zkv.reshape(b, s, 2 * width), width)
            kmean = kmeans[j] if j in kmeans else _moba_kmean(cache_moba, j, page_table)
            topk = min(MOBA_TOPK, kmean.shape[1])
            top = _moba_topk(zs4, kmean, topk)[:, :, :, 0]
            top = jnp.swapaxes(top, 1, 2).reshape(db, n_heads * topk)
            us = _moba_sample(zs4, cache_moba, j, page_table, top, topk)
            xp = out_project(up.reshape(b * s, width), w_out, j, xp)
            xs = out_project(us.reshape(db, width), w_out, j, xs)
            moba_new_p.append(zkv.reshape(b, s, 2, n_heads, HEAD_DIM))
            moba_new_s.append(zs[:, width:3 * width].reshape(db, 1, 2, n_heads, HEAD_DIM))
    y_prompt = _rmsnorm(xp, final_norm, F32).reshape(b, s, d)
    y_sample = _rmsnorm(xs, final_norm, F32).reshape(db, 1, d)
    return (y_prompt, y_sample,
            jnp.stack(dil_new_p[0]), jnp.stack(dil_new_s[0]),
            jnp.stack(dil_new_p[1]), jnp.stack(dil_new_s[1]),
            jnp.stack(dil_new_p[2]), jnp.stack(dil_new_s[2]),
            jnp.stack(moba_new_p), jnp.stack(moba_new_s))
```

```python
import functools
import math

import jax
import jax.numpy as jnp
from jax import lax
from jax.experimental import pallas as pl
from jax.experimental.pallas import tpu as pltpu

HEAD_DIM = 128
DIL_GROUPS = ((128, 1), (512, 4), (2048, 16))
MOBA_BLOCK = 256
MOBA_TOPK = 3
PAGE_SIZE = 128
ROPE_THETA = 10000.0
RMS_EPS = 1e-6
SCALE = HEAD_DIM ** -0.5
EXP2_SCALE = SCALE * math.log2(math.e)
NEG = -1e30

LANES = 128
SUBLANES = 8
MXU_COLS = 256
PAGES_PER_STEP = 8
VMEM_LIMIT = 48 * 1024 * 1024
MATMUL_VMEM_LIMIT = 56 * 1024 * 1024

F32 = jnp.float32
BF16 = jnp.bfloat16


def _params(n_axes, vmem=VMEM_LIMIT):
    return pltpu.CompilerParams(dimension_semantics=("arbitrary",) * n_axes, vmem_limit_bytes=vmem)


def _silu(g):
    return g * (1.0 / (1.0 + jnp.exp(-g)))


def _dot_nn(a, b):
    return jnp.dot(a, b, preferred_element_type=F32)


def _dot_nt(a, b, precision=None):
    return lax.dot_general(a, b, (((1,), (1,)), ((), ())), preferred_element_type=F32, precision=precision)


def _merge3(outs, lses):
    mx = jnp.maximum(jnp.maximum(lses[0], lses[1]), lses[2])
    es = [jnp.exp(x - mx) for x in lses]
    inv = 1.0 / (es[0] + es[1] + es[2])
    return (es[0] * inv) * outs[0] + (es[1] * inv) * outs[1] + (es[2] * inv) * outs[2]


def _rmsnorm_kernel(x_ref, g_ref, o_ref):
    x = x_ref[...]
    ms = jnp.mean(x * x, axis=-1, keepdims=True)
    o_ref[...] = (x * lax.rsqrt(ms + RMS_EPS) * g_ref[...]).astype(o_ref.dtype)


def _rmsnorm(x, g, out_dtype):
    m, d = x.shape
    tm = min(m, 512)
    return pl.pallas_call(
        _rmsnorm_kernel,
        out_shape=jax.ShapeDtypeStruct((m, d), out_dtype),
        grid=(m // tm,),
        in_specs=[pl.BlockSpec((tm, d), lambda i: (i, 0)), pl.BlockSpec((1, d), lambda i: (0, 0))],
        out_specs=pl.BlockSpec((tm, d), lambda i: (i, 0)),
        compiler_params=_params(1),
        name="rmsnorm",
    )(x, g.reshape(1, d))


def _rope_tables(pos):
    half = HEAD_DIM // 2
    inv = ROPE_THETA ** (-jnp.arange(half, dtype=F32) / half)
    ang = pos.astype(F32)[:, None] * inv[None, :]
    cos, sin = jnp.cos(ang), jnp.sin(ang)
    return jnp.concatenate([cos, cos], axis=-1), jnp.concatenate([-sin, sin], axis=-1)


def _rope_or_identity(pos):
    cos, sin = _rope_tables(pos)
    rot = jnp.concatenate([cos, sin], axis=-1)
    keep = jnp.concatenate([jnp.ones_like(cos), jnp.zeros_like(sin)], axis=-1)
    return jnp.stack([rot, keep])


def _matmul_kernel(*refs, rope, residual, km_steps, grid_steps):
    if km_steps:
        refs = refs[1:]
    a_ref, w_ref, *rest = refs
    wb_ref = rest.pop()
    if km_steps:
        km_ref = rest.pop()
    o_ref = rest.pop()
    if rope:
        table_ref, *rest = rest
    elif residual:
        x_ref, *rest = rest
    page_refs = rest

    @pl.when(pl.program_id(1) == 0)
    def _():
        wb_ref[...] = w_ref[...].astype(BF16)

    if km_steps:
        ppb = MOBA_BLOCK // PAGE_SIZE

        def block_means():
            for blk in range(len(page_refs) // ppb):
                acc = jnp.sum(page_refs[blk * ppb][...], axis=0)
                for p in range(1, ppb):
                    acc = acc + jnp.sum(page_refs[blk * ppb + p][...], axis=0)
                km_ref[blk] = acc * (1.0 / MOBA_BLOCK)

        if km_steps == grid_steps:
            block_means()
        else:
            pl.when(pl.program_id(0) * pl.num_programs(1) + pl.program_id(1) < km_steps)(block_means)

    a = a_ref[...].astype(BF16)
    for c in range(o_ref.shape[1] // MXU_COLS):
        cols = slice(c * MXU_COLS, (c + 1) * MXU_COLS)
        acc = _dot_nn(a, wb_ref[:, cols])
        if rope:
            cos, sin = table_ref[:, :LANES], table_ref[:, LANES:]
            for lo in range(0, MXU_COLS, HEAD_DIM):
                x = acc[:, lo:lo + HEAD_DIM]
                out = x * cos + pltpu.roll(x, HEAD_DIM // 2, 1) * sin
                o_ref[:, c * MXU_COLS + lo:c * MXU_COLS + lo + HEAD_DIM] = out
        elif residual:
            o_ref[:, cols] = x_ref[:, cols] + acc
        else:
            o_ref[:, cols] = acc


def _matmul(a, w, layer, col0, n_out, *, width, tail_col=None, tables=None, rope_cols=(), residual=None,
            pages=None, name):
    m, k = a.shape
    tm = min(m, 1024)
    tn = 1024
    off = col0 * width // tn
    n_i = m // tm
    grid = (n_out // tn, n_i)
    tail_at = (n_out - width) // tn

    def weight_tile(j, i, *_):
        skip = 0 if tail_col is None else (tail_col - (col0 + n_out // width - 1)) * (width // tn)
        return (layer, 0, off + j + (j >= tail_at).astype(jnp.int32) * skip)

    ins = [a, w]
    specs = [pl.BlockSpec((tm, k), lambda j, i, *_: (i, 0)),
             pl.BlockSpec((None, k, tn), weight_tile, pipeline_mode=pl.Buffered(1))]
    if tables is not None:
        assert tables.shape[1] % tm == 0, "row tiles must not straddle sequences"
        pos_blocks = tables.shape[1] // tm

        def table_block(j, i, *_):
            col = (j * tn) // width
            is_rope = sum((col == c).astype(jnp.int32) for c in rope_cols)
            return (1 - is_rope, i % pos_blocks, 0)

        ins.append(tables)
        specs.append(pl.BlockSpec((None, tm, 2 * LANES), table_block))
    if residual is not None:
        ins.append(residual)
        specs.append(pl.BlockSpec((tm, tn), lambda j, i, *_: (i, j)))
    out_shape = jax.ShapeDtypeStruct((m, n_out), F32)
    out_specs = pl.BlockSpec((tm, tn), lambda j, i, *_: (i, j))
    km_steps = 0
    prefetch = []
    if pages is not None:
        pool, pool_layer, page_table, first_step = pages
        db, n_pages = page_table.shape
        n_heads = pool.shape[4]
        steps_per_seq = n_pages // PAGES_PER_STEP
        km_steps = min(grid[0] * grid[1], db * steps_per_seq - first_step)
        blocks_per_step = PAGES_PER_STEP // (MOBA_BLOCK // PAGE_SIZE)

        def page(slot):
            def index(j, i, pt):
                step = first_step + jnp.minimum(j * n_i + i, km_steps - 1)
                return (pool_layer, pt[step // steps_per_seq, (step % steps_per_seq) * PAGES_PER_STEP + slot], 0, 0, 0, 0)
            return index

        prefetch = [page_table]
        ins += [pool] * PAGES_PER_STEP
        specs += [pl.BlockSpec((None, None, PAGE_SIZE, None, n_heads, HEAD_DIM), page(s)) for s in range(PAGES_PER_STEP)]
        out_shape = (out_shape, jax.ShapeDtypeStruct((km_steps * blocks_per_step, n_heads, HEAD_DIM), F32))
        out_specs = (out_specs, pl.BlockSpec((blocks_per_step, n_heads, HEAD_DIM),
                                             lambda j, i, pt: (jnp.minimum(j * n_i + i, km_steps - 1), 0, 0)))
    kern = functools.partial(_matmul_kernel, rope=tables is not None, residual=residual is not None,
                             km_steps=km_steps, grid_steps=grid[0] * grid[1])
    return pl.pallas_call(
        kern,
        out_shape=out_shape,
        grid_spec=pltpu.PrefetchScalarGridSpec(
            num_scalar_prefetch=len(prefetch), grid=grid, in_specs=specs, out_specs=out_specs,
            scratch_shapes=[pltpu.VMEM((k, tn), BF16)]),
        compiler_params=_params(2, MATMUL_VMEM_LIMIT),
        name=name,
    )(*prefetch, *ins)


def _dil_attn_kernel(z_ref, o_ref, kprev_ref, vtprev_ref, lse_s, *, n_heads, span):
    n = pl.program_id(2)
    width = n_heads * HEAD_DIM

    @pl.when(n == 0)
    def _():
        kprev_ref[...] = jnp.zeros_like(kprev_ref)
        vtprev_ref[...] = jnp.zeros_like(vtprev_ref)
        lse_s[...] = jnp.zeros_like(lse_s)

    key = lax.broadcasted_iota(jnp.int32, (2 * span, span), 0)
    qry = lax.broadcasted_iota(jnp.int32, (2 * span, span), 1)
    prev_ok = jnp.logical_and(key < span, key >= qry)
    own_ok = jnp.logical_and(key >= span, key - span <= qry)
    for sb in range(z_ref.shape[0] // span):
        rows = slice(sb * span, (sb + 1) * span)
        mask = jnp.logical_or(prev_ok if sb else jnp.logical_and(prev_ok, n > 0), own_ok)
        for h in range(n_heads):
            sl = slice(h * HEAD_DIM, (h + 1) * HEAD_DIM)
            q = z_ref[rows, sl].astype(BF16)
            kc = z_ref[rows, width + h * HEAD_DIM:width + (h + 1) * HEAD_DIM].astype(BF16)
            vtc = z_ref[rows, 2 * width + h * HEAD_DIM:2 * width + (h + 1) * HEAD_DIM].T.astype(BF16)
            s = _dot_nt(jnp.concatenate([kprev_ref[:, sl], kc], axis=0), q)
            s = jnp.where(mask, s, NEG)
            m = jnp.max(s, axis=0, keepdims=True)
            p = jnp.exp2((s - m) * EXP2_SCALE)
            l = jnp.sum(p, axis=0, keepdims=True)
            pn = (p * (1.0 / l)).astype(BF16)
            ot = _dot_nn(jnp.concatenate([vtprev_ref[h], vtc], axis=1), pn)
            o_ref[rows, sl] = ot.T
            lse_s[h:h + 1, :] = m * SCALE + jnp.log(l)
            kprev_ref[:, sl] = kc
            vtprev_ref[h] = vtc
        o_ref[rows, width:] = lse_s[...].T


def _dil_attention(z, span, width):
    b, dil, seq, _ = z.shape
    n_heads = width // HEAD_DIM
    assert span == LANES and seq % span == 0
    kern = functools.partial(_dil_attn_kernel, n_heads=n_heads, span=span)
    rows = 2 * span if seq % (2 * span) == 0 else span
    return pl.pallas_call(
        kern,
        out_shape=jax.ShapeDtypeStruct((b, dil, seq, width + LANES), F32),
        grid=(b, dil, seq // rows),
        in_specs=[pl.BlockSpec((None, None, rows, 3 * width), lambda bi, r, n: (bi, r, n, 0))],
        out_specs=pl.BlockSpec((None, None, rows, width + LANES), lambda bi, r, n: (bi, r, n, 0)),
        scratch_shapes=[pltpu.VMEM((span, width), BF16), pltpu.VMEM((n_heads, HEAD_DIM, span), BF16),
                        pltpu.VMEM((LANES, span), F32)],
        compiler_params=_params(3),
        name=f"dil_attn_d{dil}",
    )(z)


def _merge_gate_kernel(o0_ref, o1_ref, o2_ref, g_ref, u_ref, *, n_heads):
    o_refs = (o0_ref, o1_ref, o2_ref)
    width = n_heads * HEAD_DIM
    lses = [r[:, width:] for r in o_refs]
    for h in range(n_heads):
        sl = slice(h * HEAD_DIM, (h + 1) * HEAD_DIM)
        o = _merge3([r[:, sl] for r in o_refs], [x[:, h:h + 1] for x in lses])
        u_ref[:, sl] = (o * _silu(g_ref[:, sl])).astype(u_ref.dtype)


def _merge_gate(outs, gate, gate_col, width):
    m = gate.shape[0]
    tm = min(m, 256)
    n_heads = width // HEAD_DIM
    row = lambda i: (i, 0)
    kern = functools.partial(_merge_gate_kernel, n_heads=n_heads)
    return pl.pallas_call(
        kern,
        out_shape=jax.ShapeDtypeStruct((m, width), BF16),
        grid=(m // tm,),
        in_specs=[pl.BlockSpec((tm, width + LANES), row)] * 3 + [pl.BlockSpec((tm, width), lambda i: (i, gate_col))],
        out_specs=pl.BlockSpec((tm, width), row),
        compiler_params=_params(1),
        name="dil_merge_gate",
    )(*outs, gate)


def _moba_attn_kernel(q_ref, k_ref, v_ref, g_ref, u_ref, kb_ref, vt_ref, kmean_ref, sel_ref, acc_ref,
                      *, hb, nb, blk, topk):
    n = pl.program_id(2)
    heads = [slice(h * HEAD_DIM, (h + 1) * HEAD_DIM) for h in range(hb)]

    @pl.when(n == 0)
    def _():
        kb_ref[...] = k_ref[...].astype(BF16)
        kmean_ref[...] = jnp.zeros_like(kmean_ref)
        for h, sl in enumerate(heads):
            for j in range(nb):
                rows = slice(j * blk, (j + 1) * blk)
                vt_ref[h, j // 2, :, (j % 2) * blk:(j % 2 + 1) * blk] = v_ref[rows, sl].T.astype(BF16)
                kmean_ref[h, j:j + 1, :] = jnp.sum(k_ref[rows, sl], axis=0, keepdims=True) * (1.0 / blk)

    tq = 2 * blk
    nbp = kmean_ref.shape[1]
    blkid = lax.broadcasted_iota(jnp.int32, (nbp, tq), 0)
    upper = (lax.broadcasted_iota(jnp.int32, (nbp, tq), 1) >= blk).astype(jnp.int32)
    past = blkid < 2 * n + upper
    key = lax.broadcasted_iota(jnp.int32, (tq, tq), 0)
    qry = lax.broadcasted_iota(jnp.int32, (tq, tq), 1)
    causal = jnp.logical_and((key >= blk) == (qry >= blk), key <= qry)
    lower_key = key < blk
    pair = pl.ds(pl.multiple_of(n * tq, tq), tq)
    qbs, carry0 = [], []
    for h, sl in enumerate(heads):
        qf = q_ref[:, sl]
        qb = qf.astype(BF16)
        qbs.append(qb)
        gate = _dot_nt(kmean_ref[h], qf, precision=lax.Precision.HIGHEST)
        gate = jnp.where(past, gate, NEG)
        rank = jnp.zeros((nbp, tq), jnp.int32)
        for j in range(nb):
            gj = gate[j:j + 1, :]
            beats = jnp.logical_or(gj > gate, jnp.logical_and(gj == gate, blkid > j))
            rank = rank + beats.astype(jnp.int32)
        sel_ref[h] = jnp.where(jnp.logical_and(rank < topk, past), 1.0, 0.0)

        picked = sel_ref[h, pl.ds(2 * n, 1), :] > 0.0
        s = _dot_nt(kb_ref[pair, sl], qb)
        s = jnp.where(jnp.logical_or(causal, jnp.logical_and(lower_key, picked)), s, NEG)
        m0 = jnp.max(s, axis=0, keepdims=True)
        p = jnp.exp2((s - m0) * EXP2_SCALE)
        acc_ref[h] = _dot_nn(vt_ref[h, n], p.astype(BF16))
        carry0.append((m0, jnp.sum(p, axis=0, keepdims=True)))

    def body(i, carry):
        rows = pl.ds(pl.multiple_of(i * tq, tq), tq)
        out = []
        for h, sl in enumerate(heads):
            m, l = carry[h]
            s = _dot_nt(kb_ref[rows, sl], qbs[h])
            sa = jnp.where(sel_ref[h, pl.ds(2 * i, 1), :] > 0.0, s[:blk], NEG)
            sb = jnp.where(sel_ref[h, pl.ds(2 * i + 1, 1), :] > 0.0, s[blk:], NEG)
            m_new = jnp.maximum(m, jnp.maximum(jnp.max(sa, axis=0, keepdims=True), jnp.max(sb, axis=0, keepdims=True)))
            alpha = jnp.exp2((m - m_new) * EXP2_SCALE)
            pa = jnp.exp2((sa - m_new) * EXP2_SCALE)
            pb = jnp.exp2((sb - m_new) * EXP2_SCALE)
            p2 = jnp.concatenate([pa, pb], axis=0).astype(BF16)
            acc_ref[h] = alpha * acc_ref[h] + _dot_nn(vt_ref[h, i], p2)
            out.append((m_new, alpha * l + jnp.sum(pa, axis=0, keepdims=True) + jnp.sum(pb, axis=0, keepdims=True)))
        return tuple(out)

    carry = lax.fori_loop(0, n, body, tuple(carry0))
    for h, sl in enumerate(heads):
        o = (acc_ref[h] * (1.0 / carry[h][1])).T
        u_ref[:, sl] = (o * _silu(g_ref[:, sl])).astype(u_ref.dtype)


def _moba_attention(zqg, zkv, width):
    b, s, _ = zqg.shape
    blk = MOBA_BLOCK
    nb = s // blk
    assert nb % 2 == 0, "query and key blocks are processed in pairs"
    tq = 2 * blk
    nbp = -(-nb // SUBLANES) * SUBLANES
    hb = min(4, width // HEAD_DIM)
    cw = hb * HEAD_DIM
    per = width // cw
    topk = min(MOBA_TOPK, nb - 1)
    kern = functools.partial(_moba_attn_kernel, hb=hb, nb=nb, blk=blk, topk=topk)
    resident = functools.partial(pl.BlockSpec, (None, s, cw), pipeline_mode=pl.Buffered(1))
    return pl.pallas_call(
        kern,
        out_shape=jax.ShapeDtypeStruct((b, s, width), BF16),
        grid=(b, per, nb // 2),
        in_specs=[pl.BlockSpec((None, tq, cw), lambda bi, g, n: (bi, n, g)),
                  resident(lambda bi, g, n: (bi, 0, g)),
                  resident(lambda bi, g, n: (bi, 0, per + g)),
                  pl.BlockSpec((None, tq, cw), lambda bi, g, n: (bi, n, per + g))],
        out_specs=pl.BlockSpec((None, tq, cw), lambda bi, g, n: (bi, n, g)),
        scratch_shapes=[pltpu.VMEM((s, cw), BF16), pltpu.VMEM((hb, nb // 2, HEAD_DIM, tq), BF16),
                        pltpu.VMEM((hb, nbp, HEAD_DIM), F32), pltpu.VMEM((hb, nbp, tq), F32),
                        pltpu.VMEM((hb, HEAD_DIM, tq), F32)],
        compiler_params=_params(3),
        name="moba_attn",
    )(zqg, zkv, zkv, zqg)


def _dil_sample_kernel(z_ref, k0_ref, v0_ref, k1_ref, v1_ref, k2_ref, v2_ref, u_ref):
    kv = ((k0_ref, v0_ref), (k1_ref, v1_ref), (k2_ref, v2_ref))
    outs, lses = [], []
    for g, (k_ref, v_ref) in enumerate(kv):
        q, k_new, v_new = z_ref[3 * g], z_ref[3 * g + 1], z_ref[3 * g + 2]
        s = jnp.sum(k_ref[...] * q[None], axis=-1, keepdims=True) * SCALE
        s_new = jnp.sum(k_new * q, axis=-1, keepdims=True) * SCALE
        m = jnp.maximum(jnp.max(s, axis=0), s_new)
        p = jnp.exp(s - m[None])
        p_new = jnp.exp(s_new - m)
        l = jnp.sum(p, axis=0) + p_new
        outs.append((jnp.sum(p * v_ref[...], axis=0) + p_new * v_new) * (1.0 / l))
        lses.append(m + jnp.log(l))
    u_ref[...] = _merge3(outs, lses) * _silu(z_ref[9])


def _dil_sample(zs, caches, layer):
    db, _, n_heads, _ = zs.shape
    ins, specs = [zs], [pl.BlockSpec((None,) + zs.shape[1:], lambda b: (b, 0, 0, 0))]
    for (win, dil), c in zip(DIL_GROUPS, caches):
        span = win // dil
        assert c.shape[2] == win, "cache must hold a full window"
        cr = c.reshape(c.shape[0], db, span, dil, 2, n_heads, HEAD_DIM)
        blk = (None, None, span, None, None, n_heads, HEAD_DIM)
        ins += [cr, cr]
        specs += [pl.BlockSpec(blk, lambda b: (layer, b, 0, 0, 0, 0, 0)),
                  pl.BlockSpec(blk, lambda b: (layer, b, 0, 0, 1, 0, 0))]
    return pl.pallas_call(
        _dil_sample_kernel,
        out_shape=jax.ShapeDtypeStruct((db, n_heads, HEAD_DIM), F32),
        grid=(db,),
        in_specs=specs,
        out_specs=pl.BlockSpec((None, n_heads, HEAD_DIM), lambda b: (b, 0, 0)),
        compiler_params=_params(1),
        name="dil_sample",
    )(*ins)


def _kmean_kernel(pt_ref, *refs, ppb):
    k_refs, o_ref = refs[:-1], refs[-1]
    for i in range(len(k_refs) // ppb):
        acc = jnp.sum(k_refs[i * ppb][...], axis=0)
        for p in range(1, ppb):
            acc = acc + jnp.sum(k_refs[i * ppb + p][...], axis=0)
        o_ref[i] = acc * (1.0 / (ppb * PAGE_SIZE))


def _moba_kmean(pool, layer, page_table):
    db, n_pages = page_table.shape
    n_heads = pool.shape[4]
    ppb = MOBA_BLOCK // PAGE_SIZE
    pps = 8
    assert n_pages % pps == 0 and pps % ppb == 0
    blk = (None, None, PAGE_SIZE, None, n_heads, HEAD_DIM)

    def page(i):
        return lambda b, t, pt: (layer, pt[b, t * pps + i], 0, 0, 0, 0)

    return pl.pallas_call(
        functools.partial(_kmean_kernel, ppb=ppb),
        out_shape=jax.ShapeDtypeStruct((db, n_pages // ppb, n_heads, HEAD_DIM), F32),
        grid_spec=pltpu.PrefetchScalarGridSpec(
            num_scalar_prefetch=1,
            grid=(db, n_pages // pps),
            in_specs=[pl.BlockSpec(blk, page(i)) for i in range(pps)],
            out_specs=pl.BlockSpec((None, pps // ppb, n_heads, HEAD_DIM), lambda b, t, pt: (b, t, 0, 0)),
        ),
        compiler_params=_params(2),
        name="moba_kmean",
    )(page_table, *([pool] * pps))


def _moba_topk_kernel(z_ref, km_ref, o_ref, *, topk):
    n_full = km_ref.shape[0]
    gate = jnp.sum(km_ref[...] * z_ref[0][None], axis=-1, keepdims=True)
    idx = lax.broadcasted_iota(jnp.int32, gate.shape, 0).astype(F32)
    for t in range(topk):
        best = jnp.max(gate, axis=0, keepdims=True)
        pick = jnp.min(jnp.where(gate == best, idx, float(n_full)), axis=0, keepdims=True)
        o_ref[t] = jnp.broadcast_to(pick[0], o_ref.shape[1:]).astype(jnp.int32)
        gate = jnp.where(idx == pick, -jnp.inf, gate)


def _moba_topk(zs, kmean, topk):
    db, n_full, n_heads, _ = kmean.shape
    return pl.pallas_call(
        functools.partial(_moba_topk_kernel, topk=topk),
        out_shape=jax.ShapeDtypeStruct((db, topk, n_heads, LANES), jnp.int32),
        grid=(db,),
        in_specs=[pl.BlockSpec((None,) + zs.shape[1:], lambda b: (b, 0, 0, 0)),
                  pl.BlockSpec((None, n_full, n_heads, HEAD_DIM), lambda b: (b, 0, 0, 0))],
        out_specs=pl.BlockSpec((None, topk, n_heads, LANES), lambda b: (b, 0, 0, 0)),
        compiler_params=_params(1),
        name="moba_topk",
    )(zs, kmean)


def _moba_sample_kernel(pt_ref, top_ref, z_ref, pool_ref, u_ref, kbuf, vbuf, sem,
                        *, layer, n_heads, topk, ppb, n_batch):
    b = pl.program_id(0)
    slot = b % 2

    def copies(bb, sl):
        out = []
        for h in range(n_heads):
            for t in range(topk):
                first = top_ref[bb, h * topk + t] * ppb
                for p in range(ppb):
                    page = pt_ref[bb, first + p]
                    for kvi, buf in enumerate((kbuf, vbuf)):
                        out.append(pltpu.make_async_copy(pool_ref.at[layer, page, :, kvi, h, :],
                                                         buf.at[sl, h, t * ppb + p], sem.at[sl]))
        return out

    @pl.when(b == 0)
    def _():
        for c in copies(0, 0):
            c.start()

    @pl.when(b + 1 < n_batch)
    def _():
        for c in copies(b + 1, 1 - slot):
            c.start()

    for c in copies(b, slot):
        c.wait()

    for h in range(n_heads):
        q, k_new, v_new = (z_ref[c, h:h + 1, :] for c in range(3))
        k = kbuf[slot, h]
        s = jnp.sum(k * q[None], axis=-1, keepdims=True) * SCALE
        s_own = jnp.sum(k_new * q, axis=-1, keepdims=True) * SCALE
        m = jnp.maximum(jnp.max(jnp.max(s, axis=0), axis=0, keepdims=True), s_own)
        p = jnp.exp(s - m[None])
        p_own = jnp.exp(s_own - m)
        l = jnp.sum(jnp.sum(p, axis=0), axis=0, keepdims=True) + p_own
        o = jnp.sum(jnp.sum(p * vbuf[slot, h], axis=0), axis=0, keepdims=True) + p_own * v_new
        u_ref[h:h + 1, :] = o * (1.0 / l) * _silu(z_ref[3, h:h + 1, :])


def _moba_sample(zs, pool, layer, page_table, top, topk):
    db, _, n_heads, _ = zs.shape
    ppb = MOBA_BLOCK // PAGE_SIZE
    kern = functools.partial(_moba_sample_kernel, layer=layer, n_heads=n_heads, topk=topk, ppb=ppb, n_batch=db)
    buf = pltpu.VMEM((2, n_heads, topk * ppb, PAGE_SIZE, HEAD_DIM), F32)
    return pl.pallas_call(
        kern,
        out_shape=jax.ShapeDtypeStruct((db, n_heads, HEAD_DIM), F32),
        grid_spec=pltpu.PrefetchScalarGridSpec(
            num_scalar_prefetch=2,
            grid=(db,),
            in_specs=[pl.BlockSpec((None,) + zs.shape[1:], lambda b, pt, tp: (b, 0, 0, 0)),
                      pl.BlockSpec(memory_space=pl.ANY)],
            out_specs=pl.BlockSpec((None, n_heads, HEAD_DIM), lambda b, pt, tp: (b, 0, 0)),
            scratch_shapes=[buf, buf, pltpu.SemaphoreType.DMA((2,))],
        ),
        compiler_params=_params(1),
        name="moba_sample",
    )(page_table, top, zs, pool)


def kernel(x_prompt, x_sample, cache_dil0, cache_dil1, cache_dil2, cache_moba, page_table,
           norm_dil, w_in_dil, w_out_dil, norm_moba, w_in_moba, w_out_moba, final_norm):
    b, s, d = x_prompt.shape
    db, t, _ = x_sample.shape
    width = w_out_dil.shape[1]
    n_heads = width // HEAD_DIM
    depth = norm_dil.shape[0] + norm_moba.shape[0]
    n_pages = page_table.shape[1]
    assert t == 1 and n_pages % (MOBA_BLOCK // PAGE_SIZE) == 0, "decode step: one token, no partial key block"
    dil_caches = (cache_dil0, cache_dil1, cache_dil2)

    def residue_major(a, dil):
        return jnp.swapaxes(a.reshape((a.shape[0], a.shape[1] // dil, dil) + a.shape[2:]), 1, 2)

    def token_major(a):
        a = jnp.swapaxes(a, 1, 2)
        return a.reshape((a.shape[0] * a.shape[1] * a.shape[2],) + a.shape[3:])

    pos = jnp.arange(s)
    rope_p = [_rope_or_identity(residue_major(pos[None], dil).reshape(s)) for _, dil in DIL_GROUPS]
    rope_s = _rope_or_identity(jnp.full((db,), n_pages * PAGE_SIZE))
    dil_rope = tuple(c for c in range(9) if c % 3 != 2)
    project = functools.partial(_matmul, width=width, name="proj_rope")

    def out_project(u, w, layer, x):
        return _matmul(u, w, layer, 0, w.shape[2], width=width, residual=x, name="out_proj")

    xp = x_prompt.reshape(b * s, d)
    xs = x_sample.reshape(db, d)
    dil_new_p = [[] for _ in DIL_GROUPS]
    dil_new_s = [[] for _ in DIL_GROUPS]
    moba_new_p, moba_new_s = [], []
    kmeans = {}
    for i in range(depth):
        j = i // 2
        if i % 2 == 0:
            w_in, w_out = w_in_dil, w_out_dil
            n_cols = w_in.shape[2] // width
            hp = _rmsnorm(xp, norm_dil[j], BF16)
            hs = _rmsnorm(xs, norm_dil[j], F32)
            zs = project(hs, w_in, j, 0, n_cols * width, tables=rope_s, rope_cols=dil_rope)
            outs, km_parts, km_done = [], [], 0
            km_total = db * n_pages // PAGES_PER_STEP
            for g, (win, dil) in enumerate(DIL_GROUPS):
                hg = residue_major(hp.reshape(b, s, d), dil).reshape(b * s, d)
                pages = (cache_moba, j, page_table, km_done) if j < cache_moba.shape[0] and km_done < km_total else None
                n_grp, tail = (4, n_cols - 1) if dil == 1 else (3, None)
                zg = project(hg, w_in, j, 3 * g, n_grp * width, tail_col=tail, tables=rope_p[g], rope_cols=(0, 1),
                             pages=pages)
                if pages is not None:
                    zg, km = zg
                    km_parts.append(km)
                    km_done += km.shape[0] * (MOBA_BLOCK // PAGE_SIZE) // PAGES_PER_STEP
                if dil == 1:
                    gate = zg
                zg = zg.reshape(b, dil, s // dil, n_grp * width)
                outs.append(token_major(_dil_attention(zg, win // dil, width)))
                keep = min(win, s)
                kv = zg[:, :, (s - keep) // dil:, width:3 * width]
                dil_new_p[g].append(token_major(kv).reshape(b, keep, 2, n_heads, HEAD_DIM))
                kv_s = zs[:, (3 * g + 1) * width:(3 * g + 3) * width]
                dil_new_s[g].append(kv_s.reshape(db, 1, 2, n_heads, HEAD_DIM))
            up = _merge_gate(outs, gate, 3, width)
            us = _dil_sample(zs.reshape(db, n_cols, n_heads, HEAD_DIM), dil_caches, j)
            xp = out_project(up, w_out, j, xp)
            xs = out_project(us.reshape(db, width), w_out, j, xs)
            if km_parts and km_done == km_total:
                kmeans[j] = jnp.concatenate(km_parts).reshape(db, -1, n_heads, HEAD_DIM)
        else:
            w_in, w_out = w_in_moba, w_out_moba
            hp = _rmsnorm(xp, norm_moba[j], BF16)
            zqg = project(hp, w_in, j, 0, 2 * width, tail_col=3, tables=rope_p[0], rope_cols=(0,))
            zkv = project(hp, w_in, j, 1, 2 * width, tables=rope_p[0], rope_cols=(0,))
            zs = project(_rmsnorm(xs, norm_moba[j], F32), w_in, j, 0, 4 * width, tables=rope_s, rope_cols=(0, 1))
            zs4 = zs.reshape(db, 4, n_heads, HEAD_DIM)
            up = _moba_attention(zqg.reshape(b, s, 2 * width), zkv.reshape(b, s, 2 * width), width)
            kmean = kmeans[j] if j in kmeans else _moba_kmean(cache_moba, j, page_table)
            topk = min(MOBA_TOPK, kmean.shape[1])
            top = _moba_topk(zs4, kmean, topk)[:, :, :, 0]
            top = jnp.swapaxes(top, 1, 2).reshape(db, n_heads * topk)
            us = _moba_sample(zs4, cache_moba, j, page_table, top, topk)
            xp = out_project(up.reshape(b * s, width), w_out, j, xp)
            xs = out_project(us.reshape(db, width), w_out, j, xs)
            moba_new_p.append(zkv.reshape(b, s, 2, n_heads, HEAD_DIM))
            moba_new_s.append(zs[:, width:3 * width].reshape(db, 1, 2, n_heads, HEAD_DIM))
    y_prompt = _rmsnorm(xp, final_norm, F32).reshape(b, s, d)
    y_sample = _rmsnorm(xs, final_norm, F32).reshape(db, 1, d)
    return (y_prompt, y_sample,
            jnp.stack(dil_new_p[0]), jnp.stack(dil_new_s[0]),
            jnp.stack(dil_new_p[1]), jnp.stack(dil_new_s[1]),
            jnp.stack(dil_new_p[2]), jnp.stack(dil_new_s[2]),
            jnp.stack(moba_new_p), jnp.stack(moba_new_s))
```
